```python
import functools
import jax
import jax.numpy as jnp
from jax import lax
import numpy as np

D_MODEL = 1024
BATCH = 8
SEQ = 2048
DEPTH = 1
DEC_BATCH = 128
DEC_SEQ = 1
PAST_LEN = 8192
PAGE_SIZE = 128

A_HEADS = 4
A_DK = 128
A_DV = 128
A_WIDTH = A_HEADS * A_DV
HGRN_CHUNK = 64
B_HEADS = 8
B_HDIM = 64
B_WIDTH = B_HEADS * B_HDIM
IDX_HEADS = 8
IDX_DIM = 64
TOPK_MAX = 256
Q_BLOCK = 128
D_FF = ((-(-8 * D_MODEL // 3) + 255) // 256) * 256
DEEPNORM_ALPHA = (2.0 * DEPTH) ** 0.25
DEEPNORM_BETA = (8.0 * DEPTH) ** -0.25
ATTN_SCALE = B_HDIM ** -0.5
IDX_SCALE = (IDX_DIM * IDX_HEADS) ** -0.5
LN_EPS = 1e-5
RMS_EPS = 1e-6

IN_WIDTHS = (A_HEADS * A_DK, A_HEADS * A_DK, A_WIDTH, A_WIDTH, B_WIDTH, B_WIDTH, B_WIDTH,
             IDX_HEADS * IDX_DIM, IDX_DIM, IDX_HEADS, 2 * D_MODEL)
IN_SPLITS = tuple(int(s) for s in np.cumsum(IN_WIDTHS)[:-1])
D_IN = int(sum(IN_WIDTHS))

kernel_name = "hgrn2_dsa_gated_hybrid_step"


def layer_norm(x, g, b):
    xf = x.astype(jnp.float32)
    mu = xf.mean(-1, keepdims=True)
    var = jnp.square(xf - mu).mean(-1, keepdims=True)
    return ((xf - mu) * lax.rsqrt(var + LN_EPS) * g.astype(jnp.float32) + b.astype(jnp.float32)).astype(x.dtype)


def rms_norm(x, g):
    xf = x.astype(jnp.float32)
    return xf * lax.rsqrt(jnp.mean(xf * xf, -1, keepdims=True) + RMS_EPS) * g.astype(jnp.float32)


def alibi_slopes():
    return jnp.exp2(-8.0 * jnp.arange(1, B_HEADS + 1, dtype=jnp.float32) / B_HEADS)


def hgrn2_scan(q, log_f, k, v, s0):
    B, H, T, _ = q.shape
    C = min(HGRN_CHUNK, T)
    n = -(-T // C)
    pad = n * C - T
    if pad:
        padw = ((0, 0), (0, 0), (0, pad), (0, 0))
        q, log_f, k, v = (jnp.pad(a, padw) for a in (q, log_f, k, v))

    def to_chunks(a):
        return jnp.moveaxis(a.reshape(B, H, n, C, a.shape[-1]), 2, 0)

    causal = jnp.tril(jnp.ones((C, C), dtype=bool))

    def step(S, inp):
        qc, gc, kc, vc = inp
        G = jnp.cumsum(gc, axis=2)
        o_inter = jnp.einsum('bhtd,bhdv->bhtv', qc * jnp.exp(G), S)
        diff = G[:, :, :, None, :] - G[:, :, None, :, :]
        decay = jnp.exp(jnp.where(causal[None, None, :, :, None], diff, -jnp.inf))
        A = jnp.einsum('bhtd,bhsd,bhtsd->bhts', qc, kc, decay)
        o = o_inter + jnp.einsum('bhts,bhsv->bhtv', A, vc)
        G_last = G[:, :, -1, :]
        S_new = jnp.exp(G_last)[..., None] * S + jnp.einsum(
            'bhsd,bhsv->bhdv', kc * jnp.exp(G_last[:, :, None, :] - G), vc)
        return S_new, o

    S, o = lax.scan(step, s0, (to_chunks(q), to_chunks(log_f), to_chunks(k), to_chunks(v)))
    o = jnp.moveaxis(o, 0, 2).reshape(B, H, n * C, -1)[:, :, :T]
    return o, S


def indexer_topk(qidx, wts, q_pos, kidx_all, k_sel):
    L = kidx_all.shape[0]
    s = jnp.einsum('qhd,sd->qhs', qidx, kidx_all).astype(jnp.float32)
    score = jnp.einsum('qh,qhs->qs', wts.astype(jnp.float32), jax.nn.relu(s)) * IDX_SCALE
    key_pos = jnp.arange(L, dtype=jnp.int32)
    score = jnp.where(key_pos[None, :] <= q_pos[:, None], score, -jnp.inf)
    _, idx = lax.top_k(score, k_sel)
    valid = idx <= q_pos[:, None]
    return idx, valid


def attend_selected(q, q_pos, k_rows, v_rows, idx, valid, slopes):
    logits = jnp.einsum('qhd,qkhd->qhk', q, k_rows).astype(jnp.float32) * ATTN_SCALE
    dist = (q_pos[:, None] - idx).astype(jnp.float32)
    logits = logits - slopes[None, :, None] * dist[:, None, :]
    logits = jnp.where(valid[:, None, :], logits, -jnp.inf)
    p = jax.nn.softmax(logits, axis=-1)
    return jnp.einsum('qhk,qkhd->qhd', p.astype(v_rows.dtype), v_rows)


def dsa_prompt(q, k, v, qidx, kidx, wts, slopes):
    B, T = q.shape[:2]
    k_sel = min(TOPK_MAX, T // 4)
    QB = min(Q_BLOCK, T)
    nblk = T // QB
    xs = (q.reshape(B * nblk, QB, B_HEADS, B_HDIM),
          qidx.reshape(B * nblk, QB, IDX_HEADS, IDX_DIM),
          wts.reshape(B * nblk, QB, IDX_HEADS),
          jnp.repeat(jnp.arange(B, dtype=jnp.int32), nblk),
          jnp.tile(jnp.arange(T, dtype=jnp.int32).reshape(nblk, QB), (B, 1)))

    def blk(args):
        qb, qib, wb, b, pos = args
        idx, valid = indexer_topk(qib, wb, pos, kidx[b], k_sel)
        k_rows = k[b][idx]
        v_rows = v[b][idx]
        return attend_selected(qb, pos, k_rows, v_rows, idx, valid, slopes)

    out = lax.map(blk, xs)
    return out.reshape(B, T, B_WIDTH)


def dsa_sample(q, k_new, v_new, qidx, kidx_new, wts, cache_k, cache_v, cache_kidx, page_table, slopes):
    Tn = q.shape[1]
    past = page_table.shape[1] * PAGE_SIZE
    L = past + Tn
    k_sel = min(TOPK_MAX, L // 4)
    q_pos = past + jnp.arange(Tn, dtype=jnp.int32)

    def one(qs, kn, vn, qis, kin, ws, pages):
        kidx_past = cache_kidx[pages].reshape(past, IDX_DIM)
        kidx_all = jnp.concatenate([kidx_past, kin.astype(kidx_past.dtype)], axis=0)
        idx, valid = indexer_topk(qis, ws, q_pos, kidx_all, k_sel)
        in_past = (idx < past)[..., None, None]
        pidx = jnp.minimum(idx, past - 1)
        phys = pages[pidx // PAGE_SIZE]
        slot = pidx % PAGE_SIZE
        nidx = jnp.clip(idx - past, 0, Tn - 1)
        k_rows = jnp.where(in_past, cache_k[phys, slot], kn[nidx].astype(cache_k.dtype))
        v_rows = jnp.where(in_past, cache_v[phys, slot], vn[nidx].astype(cache_v.dtype))
        return attend_selected(qs, q_pos, k_rows.astype(qs.dtype), v_rows.astype(qs.dtype), idx, valid, slopes)

    out = jax.vmap(one)(q, k_new, v_new, qidx, kidx_new, wts, page_table)
    return out.reshape(q.shape[0], Tn, B_WIDTH)


def layer_forward(x, s0, attend, lb, w_in, hgrn_norm_g, w_up_a, w_up_b, w_o,
                  ln1_g, ln1_b, w_ffn_gate, w_ffn_up, w_ffn_down, ln2_g, ln2_b):
    B, T, _ = x.shape
    f32 = jnp.float32
    proj = jnp.einsum('btd,dn->btn', x, w_in)
    aq, af, ai, ag, bq, bk, bv, iq, ik, iw, gates = jnp.split(proj, IN_SPLITS, axis=-1)

    f = lb + (1.0 - lb) * jax.nn.sigmoid(af.astype(f32))
    log_f = jnp.log(f)
    k_a = 1.0 - f

    def heads(a, d):
        return a.astype(f32).reshape(B, T, A_HEADS, d).transpose(0, 2, 1, 3)

    o_a, s_new = hgrn2_scan(heads(aq, A_DK), heads(log_f, A_DK), heads(k_a, A_DK),
                            heads(ai, A_DV), s0.astype(f32))
    o_a = o_a.transpose(0, 2, 1, 3)
    o_a = rms_norm(o_a, hgrn_norm_g) * jax.nn.silu(ag.astype(f32).reshape(B, T, A_HEADS, A_DV))
    y_a = jnp.einsum('btc,cd->btd', o_a.reshape(B, T, A_WIDTH).astype(x.dtype), w_up_a)

    q = bq.reshape(B, T, B_HEADS, B_HDIM)
    k = bk.reshape(B, T, B_HEADS, B_HDIM)
    v = bv.reshape(B, T, B_HEADS, B_HDIM)
    qi = iq.reshape(B, T, IDX_HEADS, IDX_DIM)
    o_b = attend(q, k, v, qi, ik, iw)
    y_b = jnp.einsum('btc,cd->btd', o_b.astype(x.dtype), w_up_b)

    g = jax.nn.sigmoid(gates.astype(f32))
    g_a, g_b = jnp.split(g, 2, axis=-1)
    merged = (g_a * y_a.astype(f32) + g_b * y_b.astype(f32)).astype(x.dtype)
    mix = jnp.einsum('btd,de->bte', merged, w_o)
    x1 = layer_norm(DEEPNORM_ALPHA * x + mix, ln1_g, ln1_b)

    h = jax.nn.silu(jnp.einsum('btd,df->btf', x1, w_ffn_gate)) * jnp.einsum('btd,df->btf', x1, w_ffn_up)
    ffn = jnp.einsum('btf,fd->btd', h, w_ffn_down)
    x2 = layer_norm(DEEPNORM_ALPHA * x1 + ffn, ln2_g, ln2_b)
    return x2, s_new.astype(x.dtype), k, v, ik


def setup_inputs(seed: int = 0) -> dict:
    key = jax.random.key(seed)
    ks = jax.random.split(key, 24)
    f32 = jnp.float32
    n_pages = PAST_LEN // PAGE_SIZE
    n_used = DEC_BATCH * n_pages
    n_phys = n_used + n_used // 4
    beta = DEEPNORM_BETA

    def nrm(k, shape, scale):
        return jax.random.normal(k, shape, f32) * scale

    col_scale = np.concatenate([np.full((w,), beta if i in (2, 6) else 1.0, np.float32)
                                for i, w in enumerate(IN_WIDTHS)])
    page_table = jax.random.permutation(ks[6], n_phys)[:n_used].reshape(DEC_BATCH, n_pages).astype(jnp.int32)
    return {
        "x_prompt": nrm(ks[0], (BATCH, SEQ, D_MODEL), 1.0),
        "x_sample": nrm(ks[1], (DEC_BATCH, DEC_SEQ, D_MODEL), 1.0),
        "cache_k": nrm(ks[2], (DEPTH, n_phys, PAGE_SIZE, B_HEADS, B_HDIM), 1.0),
        "cache_v": nrm(ks[3], (DEPTH, n_phys, PAGE_SIZE, B_HEADS, B_HDIM), beta),
        "cache_kidx": nrm(ks[4], (DEPTH, n_phys, PAGE_SIZE, IDX_DIM), 1.0),
        "state_hgrn": nrm(ks[5], (DEPTH, DEC_BATCH, A_HEADS, A_DK, A_DV), 0.5),
        "page_table": page_table,
        "hgrn_lb_logits": nrm(ks[7], (DEPTH + 1, A_HEADS * A_DK), 0.1),
        "w_in": nrm(ks[8], (DEPTH, D_MODEL, D_IN), D_MODEL ** -0.5) * jnp.asarray(col_scale),
        "hgrn_norm_g": 1.0 + nrm(ks[9], (DEPTH, A_DV), 0.02),
        "w_up_a": nrm(ks[10], (DEPTH, A_WIDTH, D_MODEL), beta * A_WIDTH ** -0.5),
        "w_up_b": nrm(ks[11], (DEPTH, B_WIDTH, D_MODEL), beta * B_WIDTH ** -0.5),
        "w_o": nrm(ks[12], (DEPTH, D_MODEL, D_MODEL), beta * D_MODEL ** -0.5),
        "ln1_g": 1.0 + nrm(ks[13], (DEPTH, D_MODEL), 0.02),
        "ln1_b": nrm(ks[14], (DEPTH, D_MODEL), 0.02),
        "w_ffn_gate": nrm(ks[15], (DEPTH, D_MODEL, D_FF), beta * D_MODEL ** -0.5),
        "w_ffn_up": nrm(ks[16], (DEPTH, D_MODEL, D_FF), beta * D_MODEL ** -0.5),
        "w_ffn_down": nrm(ks[17], (DEPTH, D_FF, D_MODEL), beta * D_FF ** -0.5),
        "ln2_g": 1.0 + nrm(ks[18], (DEPTH, D_MODEL), 0.02),
        "ln2_b": nrm(ks[19], (DEPTH, D_MODEL), 0.02),
    }


def reference(x_prompt, x_sample, cache_k, cache_v, cache_kidx, state_hgrn, page_table,
              hgrn_lb_logits, w_in, hgrn_norm_g, w_up_a, w_up_b, w_o, ln1_g, ln1_b,
              w_ffn_gate, w_ffn_up, w_ffn_down, ln2_g, ln2_b):
    lb_all = jnp.cumsum(jax.nn.softmax(hgrn_lb_logits.astype(jnp.float32), axis=0), axis=0)
    slopes = alibi_slopes()
    xp, xs = x_prompt, x_sample
    kp_l, vp_l, kip_l, sp_l = [], [], [], []
    ks_l, vs_l, kis_l, ss_l = [], [], [], []
    for l in range(DEPTH):
        w = (w_in[l], hgrn_norm_g[l], w_up_a[l], w_up_b[l], w_o[l], ln1_g[l], ln1_b[l],
             w_ffn_gate[l], w_ffn_up[l], w_ffn_down[l], ln2_g[l], ln2_b[l])
        s0p = jnp.zeros((xp.shape[0], A_HEADS, A_DK, A_DV), jnp.float32)
        attend_p = functools.partial(dsa_prompt, slopes=slopes)
        xp, sp, kp, vp, kip = layer_forward(xp, s0p, attend_p, lb_all[l], *w)
        attend_s = functools.partial(dsa_sample, cache_k=cache_k[l], cache_v=cache_v[l],
                                     cache_kidx=cache_kidx[l], page_table=page_table, slopes=slopes)
        xs, ss, ksm, vsm, kism = layer_forward(xs, state_hgrn[l], attend_s, lb_all[l], *w)
        kp_l.append(kp); vp_l.append(vp); kip_l.append(kip); sp_l.append(sp)
        ks_l.append(ksm); vs_l.append(vsm); kis_l.append(kism); ss_l.append(ss)
    k_prompt = jnp.stack(kp_l, 0)
    v_prompt = jnp.stack(vp_l, 0)
    kidx_prompt = jnp.stack(kip_l, 0)
    hgrn_state_prompt = jnp.stack(sp_l, 0)
    k_sample = jnp.stack(ks_l, 0)
    v_sample = jnp.stack(vs_l, 0)
    kidx_sample = jnp.stack(kis_l, 0)
    hgrn_state_sample = jnp.stack(ss_l, 0)
    return (xp, xs, k_prompt, v_prompt, kidx_prompt, hgrn_state_prompt,
            k_sample, v_sample, kidx_sample, hgrn_state_sample)
```

```python
import functools

import jax
import jax.numpy as jnp
from jax import lax
from jax.experimental import pallas as pl
from jax.experimental.pallas import tpu as pltpu

F32 = jnp.float32
BF16 = jnp.bfloat16
I32 = jnp.int32

A_HEADS = 4
A_DK = 128
A_DV = 128
B_HEADS = 8
B_HDIM = 64
IDX_HEADS = 8
IDX_DIM = 64
TOPK_MAX = 256
PAGE_SIZE = 128
LN_EPS = 1e-5
RMS_EPS = 1e-6
ATTN_SCALE = B_HDIM ** -0.5
IDX_SCALE = (IDX_DIM * IDX_HEADS) ** -0.5

LANES = 128
SUBLANES = 8
VMEM_LIMIT_BYTES = 56 * 1024 * 1024

HGRN_CHUNK = 64
HGRN_BLOCK = 16
Q_BLOCK = 128
K_BLOCK = 128

INT_MIN = -2 ** 31
NEG_INF_KEY = INT_MIN + 0x007FFFFF


def _nt(a, b):
    return lax.dot_general(a, b, (((1,), (1,)), ((), ())), preferred_element_type=F32)


def _tn(a, b):
    return lax.dot_general(a, b, (((0,), (0,)), ((), ())), preferred_element_type=F32)


def _mm(a, b):
    return jnp.dot(a, b, preferred_element_type=F32)


def _const_spec(shape):
    nd = len(shape)
    return pl.BlockSpec(shape, lambda *_: (0,) * nd, pipeline_mode=pl.Buffered(1))


def _params(*sem):
    return pltpu.CompilerParams(dimension_semantics=sem, vmem_limit_bytes=VMEM_LIMIT_BYTES)


def _float_key(x):
    bits = pltpu.bitcast(x, I32)
    return bits ^ ((bits >> 31) & 0x7FFFFFFF)


def _proj_kernel(x_ref, wm_ref, wik_ref, wiw_ref, wg_ref, lb_ref,
                 aq_ref, f_ref, ai_ref, gate_ref, qb_ref, ko_ref, kb_ref, vo_ref, vt_ref,
                 qib_ref, kio_ref, ki2_ref, iwt_ref, ga_ref, gb_ref, *, tm):
    xb = x_ref[...].astype(BF16)
    w = A_HEADS * A_DK

    def mm(j):
        return _mm(xb, wm_ref[:, j * w:(j + 1) * w])

    aq_ref[...] = mm(0)
    lb = lb_ref[...]
    f_ref[...] = lb + (1.0 - lb) * jax.nn.sigmoid(mm(1))
    ai_ref[...] = mm(2)
    ag = mm(3)
    gate_ref[...] = ag * jax.nn.sigmoid(ag)
    qb_ref[...] = (mm(4) * ATTN_SCALE).astype(BF16)
    k = mm(5)
    ko_ref[...] = k
    kb_ref[...] = k.astype(BF16)
    v = mm(6)
    vo_ref[...] = v
    for r in range(tm // K_BLOCK):
        for p in range(B_HEADS // 2):
            blk = v[r * K_BLOCK:(r + 1) * K_BLOCK, p * LANES:(p + 1) * LANES]
            vt_ref[r, p] = blk.T.astype(BF16)
    qib_ref[...] = mm(7).astype(BF16)
    ki = _mm(xb, wik_ref[...])
    kio_ref[...] = ki[:, :IDX_DIM]
    ki2_ref[...] = ki.astype(BF16)
    iwt_ref[...] = _nt(wiw_ref[...], xb) * IDX_SCALE
    d = ga_ref.shape[1]
    ga_ref[...] = jax.nn.sigmoid(_mm(xb, wg_ref[:, :d]))
    gb_ref[...] = jax.nn.sigmoid(_mm(xb, wg_ref[:, d:]))


def _proj(x, wm, wik2, wiwt, wg, lb_row, *, tm):
    n, d = x.shape
    w = A_HEADS * A_DK
    nb = n // K_BLOCK
    row = lambda width: pl.BlockSpec((tm, width), lambda i: (i, 0))
    out_shape = (
        jax.ShapeDtypeStruct((n, w), F32),
        jax.ShapeDtypeStruct((n, w), F32),
        jax.ShapeDtypeStruct((n, w), F32),
        jax.ShapeDtypeStruct((n, w), F32),
        jax.ShapeDtypeStruct((n, w), BF16),
        jax.ShapeDtypeStruct((n, w), F32),
        jax.ShapeDtypeStruct((n, w), BF16),
        jax.ShapeDtypeStruct((n, w), F32),
        jax.ShapeDtypeStruct((nb, B_HEADS // 2, LANES, K_BLOCK), BF16),
        jax.ShapeDtypeStruct((n, w), BF16),
        jax.ShapeDtypeStruct((n, IDX_DIM), F32),
        jax.ShapeDtypeStruct((n, 2 * IDX_DIM), BF16),
        jax.ShapeDtypeStruct((IDX_HEADS, n), F32),
        jax.ShapeDtypeStruct((n, d), F32),
        jax.ShapeDtypeStruct((n, d), F32),
    )
    out_specs = (
        row(w), row(w), row(w), row(w), row(w), row(w), row(w), row(w),
        pl.BlockSpec((tm // K_BLOCK, B_HEADS // 2, LANES, K_BLOCK), lambda i: (i, 0, 0, 0)),
        row(w), row(IDX_DIM), row(2 * IDX_DIM),
        pl.BlockSpec((IDX_HEADS, tm), lambda i: (0, i)),
        row(d), row(d),
    )
    return pl.pallas_call(
        functools.partial(_proj_kernel, tm=tm),
        grid=(n // tm,),
        in_specs=[row(d), _const_spec(wm.shape), _const_spec(wik2.shape), _const_spec(wiwt.shape),
                  _const_spec(wg.shape), _const_spec(lb_row.shape)],
        out_specs=out_specs,
        out_shape=out_shape,
        compiler_params=_params("arbitrary"),
        name="proj",
    )(x, wm, wik2, wiwt, wg, lb_row)


def _hgrn_kernel(aq_ref, f_ref, ai_ref, gate_ref, gn_ref, o_ref, s_ref,
                 st_ref, kpad, gpad, vpad):
    C, BS = HGRN_CHUNK, HGRN_BLOCK
    c = pl.program_id(1)

    @pl.when(c == 0)
    def _():
        st_ref[...] = jnp.zeros_like(st_ref)
        kpad[...] = jnp.zeros_like(kpad)
        gpad[...] = jnp.zeros_like(gpad)
        vpad[...] = jnp.zeros_like(vpad)

    r2 = lax.broadcasted_iota(I32, (2 * C, C), 0)
    c2 = lax.broadcasted_iota(I32, (2 * C, C), 1)
    t2 = jnp.where(r2 >= C, r2 - C, r2)
    same_blk = (t2 // BS) == (c2 // BS)
    tri2 = jnp.where((c2 <= t2) & ((r2 < C) | same_blk), 1.0, 0.0).astype(BF16)

    row = lax.broadcasted_iota(I32, (C, A_DK), 0)
    blk = row // BS
    rr = lax.broadcasted_iota(I32, (C, C), 0) // BS
    cc = lax.broadcasted_iota(I32, (C, C), 1) // BS
    mask1 = (rr >= 2) & (cc < 2)
    mask2 = ((rr == 1) & (cc == 0)) | ((rr == 3) & (cc == 2))
    ones_bf = jnp.ones((A_DK, LANES), BF16)
    neg_inf = -jnp.inf

    for h in range(A_HEADS):
        sl = slice(h * A_DK, (h + 1) * A_DK)
        q = aq_ref[:, sl]
        f = f_ref[:, sl]
        v = ai_ref[:, sl]
        g = jnp.log(f)
        k = 1.0 - f
        g_hi = g.astype(BF16)
        r1 = g - g_hi.astype(F32)
        g_mid = r1.astype(BF16)
        g_lo = (r1 - g_mid.astype(F32)).astype(BF16)
        cs = _mm(tri2, jnp.concatenate([g_hi, g_mid, g_lo], axis=1))
        cs = cs[:, :A_DK] + cs[:, A_DK:2 * A_DK] + cs[:, 2 * A_DK:]
        G = cs[:C]
        Grel = cs[C:]
        g15 = G[BS - 1:BS]
        g31 = G[2 * BS - 1:2 * BS]
        g47 = G[3 * BS - 1:3 * BS]
        g_last = G[C - 1:C]

        v_bf = v.astype(BF16)
        st = st_ref[h]
        o = _nt((q * jnp.exp(G)).astype(BF16), st.astype(BF16))

        qe1 = jnp.exp(jnp.where(blk >= 2, G - g31, neg_inf))
        ke1 = jnp.exp(jnp.where(blk < 2, g31 - G, neg_inf))
        ref2 = jnp.where(blk < 2, g15, g47)
        qe2 = jnp.exp(jnp.where((blk == 1) | (blk == 3), G - ref2, neg_inf))
        ke2 = jnp.exp(jnp.where((blk == 0) | (blk == 2), ref2 - G, neg_inf))
        a1 = _nt((q * qe1).astype(BF16), (k * ke1).astype(BF16))
        a2 = _nt((q * qe2).astype(BF16), (k * ke2).astype(BF16))
        a = jnp.where(mask1, a1, 0.0) + jnp.where(mask2, a2, 0.0)
        o = o + _mm(a.astype(BF16), v_bf)

        kpad[BS:, :] = k
        gpad[BS:, :] = G
        vpad[BS:, :] = v
        for d in range(BS):
            ks = kpad[BS - d:BS - d + C, :]
            gs = gpad[BS - d:BS - d + C, :]
            vs = vpad[BS - d:BS - d + C, :]
            e = jnp.exp(jnp.where((row % BS) >= d, G - gs, neg_inf))
            rsum = _mm((q * ks * e).astype(BF16), ones_bf)
            o = o + rsum * vs

        kt = (k * jnp.exp(g_last - G)).astype(BF16)
        st_new = st * jnp.exp(g_last) + _tn(v_bf, kt)
        st_ref[h] = st_new

        @pl.when(c == pl.num_programs(1) - 1)
        def _():
            s_ref[0, h] = st_new.T

        ms = jnp.mean(o * o, axis=-1, keepdims=True)
        on = o * lax.rsqrt(ms + RMS_EPS) * gn_ref[...]
        o_ref[:, sl] = (on * gate_ref[:, sl]).astype(BF16)


def _hgrn_prompt(aq, f, ai, gate, gn_row, *, batch, seq):
    n, w = aq.shape
    nc = seq // HGRN_CHUNK
    blk = pl.BlockSpec((HGRN_CHUNK, w), lambda b, c: (b * nc + c, 0))
    pad_rows = HGRN_BLOCK + HGRN_CHUNK
    return pl.pallas_call(
        _hgrn_kernel,
        grid=(batch, nc),
        in_specs=[blk, blk, blk, blk, _const_spec(gn_row.shape)],
        out_specs=(blk, pl.BlockSpec((1, A_HEADS, A_DK, A_DV), lambda b, c: (b, 0, 0, 0))),
        out_shape=(jax.ShapeDtypeStruct((n, w), BF16),
                   jax.ShapeDtypeStruct((batch, A_HEADS, A_DK, A_DV), F32)),
        scratch_shapes=[pltpu.VMEM((A_HEADS, A_DV, A_DK), F32),
                        pltpu.VMEM((pad_rows, A_DK), F32),
                        pltpu.VMEM((pad_rows, A_DK), F32),
                        pltpu.VMEM((pad_rows, A_DV), F32)],
        compiler_params=_params("arbitrary", "arbitrary"),
        name="hgrn_prompt",
    )(aq, f, ai, gate, gn_row)


def _count_rows(key_s, nkb, hits):
    def body(kb, acc):
        off = pl.multiple_of(kb * K_BLOCK, K_BLOCK)
        hit = hits(key_s[pl.ds(off, K_BLOCK), :], kb)
        return acc + hit.reshape(K_BLOCK // SUBLANES, SUBLANES, Q_BLOCK).sum(axis=0)
    acc = lax.fori_loop(0, nkb, body, jnp.zeros((SUBLANES, Q_BLOCK), I32))
    return acc.sum(axis=0, keepdims=True)


def _attn_kernel(q_ref, qi_ref, iwt_ref, k_ref, ki2_ref, vt_ref, o_ref,
                 kaug, rq_s, rs_s, key_s, bias_s, lg_s, ot_s, *, seq, topk):
    i = pl.program_id(1)
    nkb = i + 1
    npair = B_HEADS // 2
    lane = lax.broadcasted_iota(I32, (Q_BLOCK, LANES), 1)
    krow = lax.broadcasted_iota(I32, (K_BLOCK, Q_BLOCK), 0)
    qpos = i * Q_BLOCK + lax.broadcasted_iota(I32, (K_BLOCK, Q_BLOCK), 1)

    @pl.when(i == 0)
    def _():
        def body(kb, carry):
            off = pl.multiple_of(kb * K_BLOCK, K_BLOCK)
            pos = off + lax.broadcasted_iota(I32, (K_BLOCK, LANES), 0)
            ln = lax.broadcasted_iota(I32, (K_BLOCK, LANES), 1)
            feat = jnp.where(ln == 0, (pos // 64) * 64, jnp.where(ln == 1, pos % 64, 0))
            feat = feat.astype(F32).astype(BF16)
            for p in range(npair):
                kaug[p, pl.ds(off, K_BLOCK), :LANES] = k_ref[pl.ds(off, K_BLOCK), p * LANES:(p + 1) * LANES]
                kaug[p, pl.ds(off, K_BLOCK), LANES:] = feat
            return carry
        lax.fori_loop(0, seq // K_BLOCK, body, 0)

    zero_bf = jnp.zeros((Q_BLOCK, LANES), BF16)
    for p in range(npair):
        qp = q_ref[:, p * LANES:(p + 1) * LANES]
        qip = qi_ref[:, p * LANES:(p + 1) * LANES]
        lo = lane < B_HDIM
        rq_s[p, :Q_BLOCK, :] = jnp.where(lo, qip, zero_bf)
        rq_s[p, Q_BLOCK:, :] = jnp.where(lo, zero_bf, qip)
        rs_s[p, :Q_BLOCK, :LANES] = jnp.where(lo, qp, zero_bf)
        rs_s[p, Q_BLOCK:, :LANES] = jnp.where(lo, zero_bf, qp)
        m0 = 2.0 ** -(2 * p + 1)
        m1 = 2.0 ** -(2 * p + 2)
        rs_s[p, :Q_BLOCK, LANES:] = jnp.where(lane < 2, m0, 0.0).astype(BF16)
        rs_s[p, Q_BLOCK:, LANES:] = jnp.where(lane < 2, m1, 0.0).astype(BF16)

    def score_body(kb, carry):
        off = pl.multiple_of(kb * K_BLOCK, K_BLOCK)
        kid = ki2_ref[pl.ds(off, K_BLOCK), :]
        acc = jnp.zeros((K_BLOCK, Q_BLOCK), F32)
        for p in range(npair):
            s2 = _nt(kid, rq_s[p])
            for hh in range(2):
                hd = 2 * p + hh
                acc = acc + iwt_ref[hd:hd + 1, :] * jnp.maximum(s2[:, hh * Q_BLOCK:(hh + 1) * Q_BLOCK], 0.0)
        sc = jnp.where(off + krow <= qpos, acc, -jnp.inf)
        key_s[pl.ds(off, K_BLOCK), :] = _float_key(sc)
        return carry
    lax.fori_loop(0, nkb, score_body, 0)

    def bit_body(it, ans):
        cand = ans | lax.shift_left(jnp.int32(1), 31 - it)
        cnt = _count_rows(key_s, nkb, lambda blk, kb: jnp.where(blk >= (cand ^ INT_MIN), 1, 0))
        return jnp.where(cnt >= topk, cand, ans)
    ans = lax.fori_loop(0, 32, bit_body, jnp.zeros((1, Q_BLOCK), I32))
    thr = ans ^ INT_MIN
    thr_gt = jnp.maximum(thr, NEG_INF_KEY)
    n_gt = _count_rows(key_s, nkb, lambda blk, kb: jnp.where(blk > thr_gt, 1, 0))
    need = topk - n_gt

    def tie_body(it, y):
        cand = y | lax.shift_left(jnp.int32(1), 11 - it)
        cnt = _count_rows(
            key_s, nkb,
            lambda blk, kb: jnp.where(blk == thr, jnp.where(kb * K_BLOCK + krow < cand, 1, 0), 0))
        return jnp.where(cnt < need, cand, y)
    y = lax.fori_loop(0, 12, tie_body, jnp.zeros((1, Q_BLOCK), I32))
    ylim = jnp.where(thr <= NEG_INF_KEY, 0, y + 1)

    def bias_body(kb, carry):
        off = pl.multiple_of(kb * K_BLOCK, K_BLOCK)
        blk = key_s[pl.ds(off, K_BLOCK), :]
        tie = jnp.where(blk == thr, jnp.where(off + krow < ylim, 0.0, -jnp.inf), -jnp.inf)
        bias_s[pl.ds(off, K_BLOCK), :] = jnp.where(blk > thr_gt, 0.0, tie)
        return carry
    lax.fori_loop(0, nkb, bias_body, 0)

    for p in range(npair):
        def pass1(kb, mx):
            off = pl.multiple_of(kb * K_BLOCK, K_BLOCK)
            lg = _nt(kaug[p, pl.ds(off, K_BLOCK), :], rs_s[p])
            b = bias_s[pl.ds(off, K_BLOCK), :]
            lg = lg + jnp.concatenate([b, b], axis=1)
            lg_s[pl.ds(off, K_BLOCK), :] = lg
            return jnp.maximum(mx, lg.reshape(K_BLOCK // SUBLANES, SUBLANES, 2 * Q_BLOCK).max(axis=0))
        mx = lax.fori_loop(0, nkb, pass1, jnp.full((SUBLANES, 2 * Q_BLOCK), -jnp.inf, F32))
        m = mx.max(axis=0, keepdims=True)

        ot_s[...] = jnp.zeros_like(ot_s)

        def pass2(kb, l8):
            off = pl.multiple_of(kb * K_BLOCK, K_BLOCK)
            pe = jnp.exp(lg_s[pl.ds(off, K_BLOCK), :] - m)
            ot_s[...] += _mm(vt_ref[kb, p], pe.astype(BF16))
            return l8 + pe.reshape(K_BLOCK // SUBLANES, SUBLANES, 2 * Q_BLOCK).sum(axis=0)
        l8 = lax.fori_loop(0, nkb, pass2, jnp.zeros((SUBLANES, 2 * Q_BLOCK), F32))
        inv = 1.0 / l8.sum(axis=0, keepdims=True)
        ot = ot_s[...]
        top = (ot[:B_HDIM, :Q_BLOCK] * inv[:, :Q_BLOCK]).T
        bot = (ot[B_HDIM:, Q_BLOCK:] * inv[:, Q_BLOCK:]).T
        o_ref[:, p * LANES:(p + 1) * LANES] = jnp.concatenate([top, bot], axis=1).astype(BF16)


def _attn_prompt(qb, qib, iwt, kb, ki2, vt, *, batch, seq, topk):
    n, w = qb.shape
    nq = seq // Q_BLOCK
    npair = B_HEADS // 2
    qspec = pl.BlockSpec((Q_BLOCK, w), lambda b, i: (b * nq + i, 0))
    return pl.pallas_call(
        functools.partial(_attn_kernel, seq=seq, topk=topk),
        grid=(batch, nq),
        in_specs=[qspec, qspec,
                  pl.BlockSpec((IDX_HEADS, Q_BLOCK), lambda b, i: (0, b * nq + i)),
                  pl.BlockSpec((seq, w), lambda b, i: (b, 0)),
                  pl.BlockSpec((seq, 2 * IDX_DIM), lambda b, i: (b, 0)),
                  pl.BlockSpec((seq // K_BLOCK, npair, LANES, K_BLOCK), lambda b, i: (b, 0, 0, 0))],
        out_specs=qspec,
        out_shape=jax.ShapeDtypeStruct((n, w), BF16),
        scratch_shapes=[pltpu.VMEM((npair, seq, 2 * LANES), BF16),
                        pltpu.VMEM((npair, 2 * Q_BLOCK, LANES), BF16),
                        pltpu.VMEM((npair, 2 * Q_BLOCK, 2 * LANES), BF16),
                        pltpu.VMEM((seq, Q_BLOCK), I32),
                        pltpu.VMEM((seq, Q_BLOCK), F32),
                        pltpu.VMEM((seq, 2 * Q_BLOCK), F32),
                        pltpu.VMEM((LANES, 2 * Q_BLOCK), F32)],
        compiler_params=_params("arbitrary", "arbitrary"),
        name="attn_prompt",
    )(qb, qib, iwt, kb, ki2, vt)


def _layer_norm(x, g, b):
    mu = jnp.mean(x, axis=-1, keepdims=True)
    xc = x - mu
    var = jnp.mean(xc * xc, axis=-1, keepdims=True)
    return xc * lax.rsqrt(var + LN_EPS) * g + b


def _final_kernel(x_ref, oa_ref, ob_ref, ga_ref, gb_ref, wua_ref, wub_ref, wo_ref,
                  l1g_ref, l1b_ref, wfg_ref, wfu_ref, wfd_ref, l2g_ref, l2b_ref, y_ref,
                  *, alpha, ff_chunk):
    ya = _mm(oa_ref[...], wua_ref[...])
    yb = _mm(ob_ref[...], wub_ref[...])
    merged = (ga_ref[...] * ya + gb_ref[...] * yb).astype(BF16)
    mix = _mm(merged, wo_ref[...])
    x1 = _layer_norm(alpha * x_ref[...] + mix, l1g_ref[...], l1b_ref[...])
    x1b = x1.astype(BF16)
    d_ff = wfg_ref.shape[1]
    acc = jnp.zeros(x1.shape, F32)
    for c in range(d_ff // ff_chunk):
        cs = slice(c * ff_chunk, (c + 1) * ff_chunk)
        hg = _mm(x1b, wfg_ref[:, cs])
        hu = _mm(x1b, wfu_ref[:, cs])
        hc = (hg * jax.nn.sigmoid(hg) * hu).astype(BF16)
        acc = acc + _mm(hc, wfd_ref[cs, :])
    y_ref[...] = _layer_norm(alpha * x1 + acc, l2g_ref[...], l2b_ref[...])


def _final(x, oa, ob, ga, gb, wua, wub, wo, l1g, l1b, wfg, wfu, wfd, l2g, l2b, *, tm, alpha):
    n, d = x.shape
    w = oa.shape[1]
    row = lambda width: pl.BlockSpec((tm, width), lambda i: (i, 0))
    consts = (wua, wub, wo, l1g, l1b, wfg, wfu, wfd, l2g, l2b)
    return pl.pallas_call(
        functools.partial(_final_kernel, alpha=alpha, ff_chunk=256),
        grid=(n // tm,),
        in_specs=[row(d), row(w), row(w), row(d), row(d)] + [_const_spec(a.shape) for a in consts],
        out_specs=row(d),
        out_shape=jax.ShapeDtypeStruct((n, d), F32),
        compiler_params=_params("arbitrary"),
        name="merge_ffn",
    )(x, oa, ob, ga, gb, *consts)


def _hgrn_step_kernel(qt_ref, ft_ref, v_ref, gate_ref, gn_ref, s_ref, so_ref, o_ref):
    b = pl.program_id(0)
    nb = qt_ref.shape[1]
    lane = lax.broadcasted_iota(I32, (A_HEADS * A_DK, nb), 1)
    sel = lane == b
    qcol = jnp.sum(jnp.where(sel, qt_ref[...], 0.0), axis=1, keepdims=True)
    fcol = jnp.sum(jnp.where(sel, ft_ref[...], 0.0), axis=1, keepdims=True)
    for h in range(A_HEADS):
        sl = slice(h * A_DK, (h + 1) * A_DK)
        fc = fcol[sl]
        vrow = v_ref[0, :, sl]
        s_new = fc * s_ref[0, h] + (1.0 - fc) * vrow
        so_ref[0, h] = s_new
        o = jnp.sum(qcol[sl] * s_new, axis=0, keepdims=True)
        ms = jnp.mean(o * o, axis=-1, keepdims=True)
        on = o * lax.rsqrt(ms + RMS_EPS) * gn_ref[...]
        o_ref[0, :, sl] = (on * gate_ref[0, :, sl]).astype(BF16)


def _hgrn_step(aq_t, f_t, ai3, gate3, gn_row, state):
    nb = state.shape[0]
    w = A_HEADS * A_DK
    sspec = pl.BlockSpec((1, A_HEADS, A_DK, A_DV), lambda b: (b, 0, 0, 0))
    rspec = pl.BlockSpec((1, 1, w), lambda b: (b, 0, 0))
    return pl.pallas_call(
        _hgrn_step_kernel,
        grid=(nb,),
        in_specs=[_const_spec(aq_t.shape), _const_spec(f_t.shape), rspec, rspec,
                  _const_spec(gn_row.shape), sspec],
        out_specs=(sspec, rspec),
        out_shape=(jax.ShapeDtypeStruct(state.shape, F32),
                   jax.ShapeDtypeStruct((nb, 1, w), BF16)),
        compiler_params=_params("arbitrary"),
        name="hgrn_step",
    )(aq_t, f_t, ai3, gate3, gn_row, state)


def _page_copies(pt_ref, cache_ref, buf, sem, b, slot, n_pages, layer):
    def body(pg, carry):
        phys = pt_ref[b, pg]
        pltpu.make_async_copy(cache_ref.at[layer, phys],
                              buf.at[slot, pl.ds(pl.multiple_of(pg * PAGE_SIZE, PAGE_SIZE), PAGE_SIZE), :],
                              sem.at[slot]).start()
        return carry
    lax.fori_loop(0, n_pages, body, 0)


def _idx_score_kernel(pt_ref, qi_ref, iwt_ref, kn_ref, cache_ref, sc_ref, sn_ref, buf, sem,
                      *, n_pages, chunk, layer):
    b = pl.program_id(0)
    nb = pl.num_programs(0)
    slot = b % 2

    @pl.when(b == 0)
    def _():
        _page_copies(pt_ref, cache_ref, buf, sem, b, slot, n_pages, layer)

    @pl.when(b + 1 < nb)
    def _():
        _page_copies(pt_ref, cache_ref, buf, sem, b + 1, 1 - slot, n_pages, layer)

    qi = qi_ref[0]
    lane = lax.broadcasted_iota(I32, iwt_ref.shape, 1)
    wcol = jnp.sum(jnp.where(lane == b, iwt_ref[...], 0.0), axis=1, keepdims=True)

    sn = jnp.sum(qi.astype(F32) * kn_ref[0], axis=1, keepdims=True)
    sn = jnp.sum(wcol * jnp.maximum(sn, 0.0), axis=0, keepdims=True)
    sn_ref[0] = jnp.broadcast_to(sn, sn_ref.shape[1:])

    pltpu.make_async_copy(buf.at[slot], buf.at[slot], sem.at[slot]).wait()
    past = n_pages * PAGE_SIZE
    for c in range(past // chunk):
        kc = buf[slot, c * chunk:(c + 1) * chunk, :].astype(BF16)
        s = _nt(qi, kc)
        sc_ref[0, :, c * chunk:(c + 1) * chunk] = jnp.sum(wcol * jnp.maximum(s, 0.0), axis=0, keepdims=True)


def _idx_scores(page_table, qi3, iwt, kidx_new3, cache_kidx, *, layer):
    nb, n_pages = page_table.shape
    past = n_pages * PAGE_SIZE
    grid_spec = pltpu.PrefetchScalarGridSpec(
        num_scalar_prefetch=1,
        grid=(nb,),
        in_specs=[pl.BlockSpec((1, IDX_HEADS, IDX_DIM), lambda b, pt: (b, 0, 0)),
                  _const_spec(iwt.shape),
                  pl.BlockSpec((1, 1, IDX_DIM), lambda b, pt: (b, 0, 0)),
                  pl.BlockSpec(memory_space=pl.ANY)],
        out_specs=(pl.BlockSpec((1, 1, past), lambda b, pt: (b, 0, 0)),
                   pl.BlockSpec((1, 1, LANES), lambda b, pt: (b, 0, 0))),
        scratch_shapes=[pltpu.VMEM((2, past, IDX_DIM), F32),
                        pltpu.SemaphoreType.DMA((2,))],
    )
    return pl.pallas_call(
        functools.partial(_idx_score_kernel, n_pages=n_pages, chunk=min(past, 1024), layer=layer),
        grid_spec=grid_spec,
        out_shape=(jax.ShapeDtypeStruct((nb, 1, past), F32),
                   jax.ShapeDtypeStruct((nb, 1, LANES), F32)),
        compiler_params=_params("arbitrary"),
        name="idx_scores",
    )(page_table, qi3, iwt, kidx_new3, cache_kidx)


def _select_kernel(sc_ref, sn_ref, mask_ref, newsel_ref, key_s, *, topk, pos_bits):
    nb, past = sc_ref.shape
    ncol = past // LANES
    key_s[...] = _float_key(sc_ref[...])
    kn = _float_key(sn_ref[...])
    pos = lax.broadcasted_iota(I32, (nb, LANES), 1)

    def count(hits_past, hit_new):
        def body(j, acc):
            off = pl.multiple_of(j * LANES, LANES)
            return acc + hits_past(key_s[:, pl.ds(off, LANES)], off)
        acc = lax.fori_loop(0, ncol, body, jnp.zeros((nb, LANES), I32))
        tot = jnp.sum(acc, axis=1, keepdims=True)
        return jnp.broadcast_to(tot, (nb, LANES)) + hit_new

    def bit_body(it, ans):
        cand = ans | lax.shift_left(jnp.int32(1), 31 - it)
        c = cand ^ INT_MIN
        cnt = count(lambda blk, off: jnp.where(blk >= c, 1, 0), jnp.where(kn >= c, 1, 0))
        return jnp.where(cnt >= topk, cand, ans)
    ans = lax.fori_loop(0, 32, bit_body, jnp.zeros((nb, LANES), I32))
    thr = ans ^ INT_MIN
    thr_gt = jnp.maximum(thr, NEG_INF_KEY)
    need = topk - count(lambda blk, off: jnp.where(blk > thr_gt, 1, 0), jnp.where(kn > thr_gt, 1, 0))

    def tie_body(it, y):
        cand = y | lax.shift_left(jnp.int32(1), pos_bits - 1 - it)
        cnt = count(lambda blk, off: jnp.where(blk == thr, jnp.where(off + pos < cand, 1, 0), 0),
                    jnp.where(kn == thr, jnp.where(past < cand, 1, 0), 0))
        return jnp.where(cnt < need, cand, y)
    y = lax.fori_loop(0, pos_bits, tie_body, jnp.zeros((nb, LANES), I32))
    ylim = jnp.where(thr <= NEG_INF_KEY, 0, y + 1)

    def out_body(j, carry):
        off = pl.multiple_of(j * LANES, LANES)
        blk = key_s[:, pl.ds(off, LANES)]
        tie = jnp.where(blk == thr, jnp.where(off + pos < ylim, 1.0, 0.0), 0.0)
        mask_ref[:, pl.ds(off, LANES)] = jnp.where(blk > thr_gt, 1.0, tie).astype(BF16)
        return carry
    lax.fori_loop(0, ncol, out_body, 0)
    tie_new = jnp.where(kn == thr, jnp.where(past < ylim, 1, 0), 0)
    newsel_ref[...] = jnp.where(kn > thr_gt, 1, tie_new)


def _select(scores, snew, *, topk):
    nb, past = scores.shape
    pos_bits = (past + 1).bit_length()
    return pl.pallas_call(
        functools.partial(_select_kernel, topk=topk, pos_bits=pos_bits),
        out_shape=(jax.ShapeDtypeStruct((nb, past), BF16),
                   jax.ShapeDtypeStruct((nb, LANES), I32)),
        scratch_shapes=[pltpu.VMEM((nb, past), I32)],
        compiler_params=pltpu.CompilerParams(vmem_limit_bytes=VMEM_LIMIT_BYTES),
        name="topk_select",
    )(scores, snew)


def _compact_kernel(m_ref, pos_ref, cnt_ref, *, topk):
    m = m_ref[0]
    n_pages = m.shape[0]
    si = lax.broadcasted_iota(I32, (PAGE_SIZE, PAGE_SIZE), 0)
    sj = lax.broadcasted_iota(I32, (PAGE_SIZE, PAGE_SIZE), 1)
    upper = jnp.where(si <= sj, 1.0, 0.0).astype(BF16)
    incl = _mm(m, upper)
    tot = _mm(m, jnp.ones((PAGE_SIZE, LANES), BF16))
    pi = lax.broadcasted_iota(I32, (n_pages, n_pages), 0)
    pj = lax.broadcasted_iota(I32, (n_pages, n_pages), 1)
    strict = jnp.where(pj < pi, 1.0, 0.0).astype(BF16)
    excl = _mm(strict, tot.astype(BF16))
    incl_pg = excl + tot
    r = lax.broadcasted_iota(I32, (1, topk), 1).astype(F32)
    reps = topk // LANES
    tile = lambda a: jnp.concatenate([a] * reps, axis=1)
    page_of = jnp.sum(jnp.where(tile(incl_pg) <= r, 1.0, 0.0), axis=0, keepdims=True)
    prow = lax.broadcasted_iota(I32, (n_pages, topk), 0).astype(F32)
    onehot = prow == page_of
    excl_r = jnp.sum(jnp.where(onehot, tile(excl), 0.0), axis=0, keepdims=True)
    want = r - excl_r + 1.0
    ranked = (m.astype(F32) * incl).astype(BF16)
    w = _tn(ranked, jnp.where(onehot, 1.0, 0.0).astype(BF16))
    srow = lax.broadcasted_iota(I32, (PAGE_SIZE, topk), 0).astype(F32)
    slot = jnp.sum(jnp.where(w == want, srow, 0.0), axis=0, keepdims=True)
    total = incl_pg[n_pages - 1:n_pages, :]
    valid = r < tile(total)
    pos = jnp.where(valid, page_of * PAGE_SIZE + slot, 0.0)
    pos_ref[0] = pos.astype(I32)
    cnt_ref[0] = total.astype(I32)


def _compact(mask3, *, topk):
    nb, n_pages, _ = mask3.shape
    return pl.pallas_call(
        functools.partial(_compact_kernel, topk=topk),
        grid=(nb,),
        in_specs=[pl.BlockSpec((1, n_pages, PAGE_SIZE), lambda b: (b, 0, 0))],
        out_specs=(pl.BlockSpec((1, 1, topk), lambda b: (b, 0, 0)),
                   pl.BlockSpec((1, 1, LANES), lambda b: (b, 0, 0))),
        out_shape=(jax.ShapeDtypeStruct((nb, 1, topk), I32),
                   jax.ShapeDtypeStruct((nb, 1, LANES), I32)),
        compiler_params=_params("arbitrary"),
        name="compact",
    )(mask3)


def _row_copies(pt_ref, pos_ref, ck_ref, cv_ref, kbuf, vbuf, sem, b, slot, topk, layer):
    def body(r, carry):
        p = pos_ref[b, r]
        phys = pt_ref[b, lax.shift_right_logical(p, 7)]
        sl = p & (PAGE_SIZE - 1)
        dst = pl.ds(pl.multiple_of(r * B_HEADS, B_HEADS), B_HEADS)
        pltpu.make_async_copy(ck_ref.at[layer, phys, sl], kbuf.at[slot, dst, :], sem.at[0, slot]).start()
        pltpu.make_async_copy(cv_ref.at[layer, phys, sl], vbuf.at[slot, dst, :], sem.at[1, slot]).start()
        return carry
    lax.fori_loop(0, topk, body, 0)


def _gather_attn_kernel(pt_ref, pos_ref, cnt_ref, nsel_ref, q_ref, kn_ref, vn_ref, posv_ref,
                        ck_ref, cv_ref, o_ref, kbuf, vbuf, sem, *, topk, past, layer):
    b = pl.program_id(0)
    nb = pl.num_programs(0)
    slot = b % 2

    @pl.when(b == 0)
    def _():
        _row_copies(pt_ref, pos_ref, ck_ref, cv_ref, kbuf, vbuf, sem, b, slot, topk, layer)

    @pl.when(b + 1 < nb)
    def _():
        _row_copies(pt_ref, pos_ref, ck_ref, cv_ref, kbuf, vbuf, sem, b + 1, 1 - slot, topk, layer)

    q = q_ref[0]
    qb = q.astype(BF16)
    hrow = lax.broadcasted_iota(I32, (B_HEADS, topk), 0)
    slopes = jnp.exp2(-(lax.broadcasted_iota(I32, (B_HEADS, 1), 0) + 1).astype(F32))
    r = lax.broadcasted_iota(I32, (1, topk), 1)
    dist = (past - posv_ref[0]).astype(F32)
    valid = r < cnt_ref[b]

    pltpu.make_async_copy(kbuf.at[slot], kbuf.at[slot], sem.at[0, slot]).wait()
    pltpu.make_async_copy(vbuf.at[slot], vbuf.at[slot], sem.at[1, slot]).wait()

    lg = jnp.zeros((B_HEADS, topk), F32)
    for h in range(B_HEADS):
        kh = kbuf[slot, pl.ds(h, topk, stride=B_HEADS), :].astype(BF16)
        lg = jnp.where(hrow == h, _nt(qb, kh), lg)
    lg = jnp.where(valid, lg - slopes * dist, -jnp.inf)
    lgn = jnp.sum(qb.astype(F32) * kn_ref[0].astype(BF16).astype(F32), axis=1, keepdims=True)
    lgn = jnp.where(nsel_ref[b] > 0, lgn, -jnp.inf)
    m = jnp.maximum(jnp.max(lg, axis=1, keepdims=True), lgn)
    pe = jnp.exp(lg - m)
    pn = jnp.exp(lgn - m)
    l = jnp.sum(pe, axis=1, keepdims=True) + pn
    peb = pe.astype(BF16)
    hrow_o = lax.broadcasted_iota(I32, (B_HEADS, B_HDIM), 0)
    o = pn.astype(BF16).astype(F32) * vn_ref[0].astype(BF16).astype(F32)
    for h in range(B_HEADS):
        vh = vbuf[slot, pl.ds(h, topk, stride=B_HEADS), :].astype(BF16)
        o = o + jnp.where(hrow_o == h, _mm(peb, vh), 0.0)
    o_ref[0] = (o / l).astype(BF16)


def _gather_attn(page_table, pos2, cnt, nsel, q3, kn3, vn3, pos3, cache_k, cache_v, *, topk, layer):
    nb, n_pages = page_table.shape
    past = n_pages * PAGE_SIZE
    hspec = pl.BlockSpec((1, B_HEADS, B_HDIM), lambda b, *_: (b, 0, 0))
    grid_spec = pltpu.PrefetchScalarGridSpec(
        num_scalar_prefetch=4,
        grid=(nb,),
        in_specs=[hspec, hspec, hspec,
                  pl.BlockSpec((1, 1, topk), lambda b, *_: (b, 0, 0)),
                  pl.BlockSpec(memory_space=pl.ANY),
                  pl.BlockSpec(memory_space=pl.ANY)],
        out_specs=hspec,
        scratch_shapes=[pltpu.VMEM((2, topk * B_HEADS, B_HDIM), F32),
                        pltpu.VMEM((2, topk * B_HEADS, B_HDIM), F32),
                        pltpu.SemaphoreType.DMA((2, 2))],
    )
    return pl.pallas_call(
        functools.partial(_gather_attn_kernel, topk=topk, past=past, layer=layer),
        grid_spec=grid_spec,
        out_shape=jax.ShapeDtypeStruct((nb, B_HEADS, B_HDIM), BF16),
        compiler_params=_params("arbitrary"),
        name="gather_attn",
    )(page_table, pos2, cnt, nsel, q3, kn3, vn3, pos3, cache_k, cache_v)


def _prep_weights(lb, w_in, hgrn_norm_g, w_up_a, w_up_b, w_o, ln1_g, ln1_b,
                  w_ffn_gate, w_ffn_up, w_ffn_down, ln2_g, ln2_b):
    w = A_HEADS * A_DK
    n_main = 8 * w
    o_ik = n_main
    o_iw = o_ik + IDX_DIM
    o_g = o_iw + IDX_HEADS
    wb = w_in.astype(BF16)
    wik = wb[:, o_ik:o_iw]
    row = lambda a: a.reshape(1, -1).astype(F32)
    return dict(
        wm=wb[:, :n_main],
        wik2=jnp.concatenate([wik, wik], axis=1),
        wiwt=wb[:, o_iw:o_g].T,
        wg=wb[:, o_g:],
        lb=row(lb),
        gn=row(hgrn_norm_g),
        wua=w_up_a.astype(BF16), wub=w_up_b.astype(BF16), wo=w_o.astype(BF16),
        l1g=row(ln1_g), l1b=row(ln1_b),
        wfg=w_ffn_gate.astype(BF16), wfu=w_ffn_up.astype(BF16), wfd=w_ffn_down.astype(BF16),
        l2g=row(ln2_g), l2b=row(ln2_b),
    )


def _final_call(x2, oa, ob, ga, gb, pw, *, tm, alpha):
    return _final(x2, oa, ob, ga, gb, pw["wua"], pw["wub"], pw["wo"], pw["l1g"], pw["l1b"],
                  pw["wfg"], pw["wfu"], pw["wfd"], pw["l2g"], pw["l2b"], tm=tm, alpha=alpha)


def _prompt_layer(x, pw, *, alpha):
    batch, seq, d = x.shape
    n = batch * seq
    x2 = x.reshape(n, d)
    (aq, f, ai, gate, qb, k_out, kb, v_out, vt, qib, kidx_out, ki2, iwt, ga, gb) = _proj(
        x2, pw["wm"], pw["wik2"], pw["wiwt"], pw["wg"], pw["lb"], tm=256)
    oa, s_new = _hgrn_prompt(aq, f, ai, gate, pw["gn"], batch=batch, seq=seq)
    ob = _attn_prompt(qb, qib, iwt, kb, ki2, vt, batch=batch, seq=seq, topk=min(TOPK_MAX, seq // 4))
    y = _final_call(x2, oa, ob, ga, gb, pw, tm=256, alpha=alpha)
    return (y.reshape(batch, seq, d), s_new,
            k_out.reshape(batch, seq, B_HEADS, B_HDIM), v_out.reshape(batch, seq, B_HEADS, B_HDIM),
            kidx_out.reshape(batch, seq, IDX_DIM))


def _sample_layer(x, state, cache_k, cache_v, cache_kidx, page_table, pw, *, alpha, layer):
    nb, tn, d = x.shape
    n_pages = page_table.shape[1]
    past = n_pages * PAGE_SIZE
    topk = min(TOPK_MAX, (past + tn) // 4)
    w = A_HEADS * A_DK
    x2 = x.reshape(nb, d)
    (aq, f, ai, gate, qb, k_out, kb, v_out, vt, qib, kidx_out, ki2, iwt, ga, gb) = _proj(
        x2, pw["wm"], pw["wik2"], pw["wiwt"], pw["wg"], pw["lb"], tm=nb)
    s_new, oa3 = _hgrn_step(aq.T, f.T, ai.reshape(nb, 1, w), gate.reshape(nb, 1, w), pw["gn"], state)
    scores, snew = _idx_scores(page_table, qib.reshape(nb, IDX_HEADS, IDX_DIM), iwt,
                               kidx_out.reshape(nb, 1, IDX_DIM), cache_kidx, layer=layer)
    mask, newsel = _select(scores.reshape(nb, past), snew.reshape(nb, LANES), topk=topk)
    pos3, cnt3 = _compact(mask.reshape(nb, n_pages, PAGE_SIZE), topk=topk)
    ob3 = _gather_attn(page_table, pos3.reshape(nb, topk), cnt3[:, 0, 0], newsel[:, 0],
                       qb.astype(F32).reshape(nb, B_HEADS, B_HDIM),
                       k_out.reshape(nb, B_HEADS, B_HDIM), v_out.reshape(nb, B_HEADS, B_HDIM),
                       pos3, cache_k, cache_v, topk=topk, layer=layer)
    y = _final_call(x2, oa3.reshape(nb, w), ob3.reshape(nb, w), ga, gb, pw, tm=nb, alpha=alpha)
    return (y.reshape(nb, tn, d), s_new,
            k_out.reshape(nb, tn, B_HEADS, B_HDIM), v_out.reshape(nb, tn, B_HEADS, B_HDIM),
            kidx_out.reshape(nb, tn, IDX_DIM))


def kernel(x_prompt, x_sample, cache_k, cache_v, cache_kidx, state_hgrn, page_table, hgrn_lb_logits,
           w_in, hgrn_norm_g, w_up_a, w_up_b, w_o, ln1_g, ln1_b, w_ffn_gate, w_ffn_up, w_ffn_down,
           ln2_g, ln2_b):
    depth = w_in.shape[0]
    alpha = (2.0 * depth) ** 0.25
    lb_all = jnp.cumsum(jax.nn.softmax(hgrn_lb_logits.astype(F32), axis=0), axis=0)
    xp, xs = x_prompt, x_sample
    outs_p, outs_s = [], []
    for l in range(depth):
        pw = _prep_weights(lb_all[l], w_in[l], hgrn_norm_g[l], w_up_a[l], w_up_b[l], w_o[l],
                           ln1_g[l], ln1_b[l], w_ffn_gate[l], w_ffn_up[l], w_ffn_down[l],
                           ln2_g[l], ln2_b[l])
        xp, sp, kp, vp, kip = _prompt_layer(xp, pw, alpha=alpha)
        xs, ss, ks, vs, kis = _sample_layer(xs, state_hgrn[l], cache_k, cache_v, cache_kidx,
                                            page_table, pw, alpha=alpha, layer=l)
        outs_p.append((kp, vp, kip, sp))
        outs_s.append((ks, vs, kis, ss))
    stack = lambda outs, j: jnp.stack([o[j] for o in outs], 0)
    return (xp, xs, stack(outs_p, 0), stack(outs_p, 1), stack(outs_p, 2), stack(outs_p, 3),
            stack(outs_s, 0), stack(outs_s, 1), stack(outs_s, 2), stack(outs_s, 3))
```

```python
import functools

import jax
import jax.numpy as jnp
from jax import lax
from jax.experimental import pallas as pl
from jax.experimental.pallas import tpu as pltpu

F32 = jnp.float32
BF16 = jnp.bfloat16
I32 = jnp.int32

A_HEADS = 4
A_DK = 128
A_DV = 128
B_HEADS = 8
B_HDIM = 64
IDX_HEADS = 8
IDX_DIM = 64
TOPK_MAX = 256
PAGE_SIZE = 128
LN_EPS = 1e-5
RMS_EPS = 1e-6
ATTN_SCALE = B_HDIM ** -0.5
IDX_SCALE = (IDX_DIM * IDX_HEADS) ** -0.5

LANES = 128
SUBLANES = 8
VMEM_LIMIT_BYTES = 56 * 1024 * 1024

HGRN_CHUNK = 64
HGRN_BLOCK = 16
Q_BLOCK = 128
K_BLOCK = 128
KB_UNROLL = 2
PAGE_RING = 16

INT_MIN = -2 ** 31
NEG_INF_KEY = INT_MIN + 0x007FFFFF


def _nt(a, b):
    return lax.dot_general(a, b, (((1,), (1,)), ((), ())), preferred_element_type=F32)


def _tn(a, b):
    return lax.dot_general(a, b, (((0,), (0,)), ((), ())), preferred_element_type=F32)


def _mm(a, b):
    return jnp.dot(a, b, preferred_element_type=F32)


def _const_spec(shape):
    nd = len(shape)
    return pl.BlockSpec(shape, lambda *_: (0,) * nd, pipeline_mode=pl.Buffered(1))


def _params(*sem):
    return pltpu.CompilerParams(dimension_semantics=sem, vmem_limit_bytes=VMEM_LIMIT_BYTES)


def _float_key(x):
    bits = pltpu.bitcast(x, I32)
    return bits ^ ((bits >> 31) & 0x7FFFFFFF)


def _lane_column(x, b):
    lane = lax.broadcasted_iota(I32, x.shape, 1)
    return jnp.sum(jnp.where(lane == b, x, 0.0), axis=1, keepdims=True)


def _split3(x):
    hi = x.astype(BF16)
    r = x - hi.astype(F32)
    mid = r.astype(BF16)
    lo = (r - mid.astype(F32)).astype(BF16)
    return hi, mid, lo


def _sum6(hh, hm, mh, hl, lh, mm):
    return hh + ((hm + mh) + ((hl + lh) + mm))


def _proj_kernel(x_ref, wm_ref, wkvt_ref, wikt_ref, wacc_ref, wiwh_ref, wiwm_ref, wiwl_ref, wg_ref, lb_ref,
                 aq_ref, f_ref, ai_ref, gate_ref, qb_ref, kt_ref, kb_ref, vt_ref, vtb_ref,
                 kit_ref, qia_ref, kia_ref, iwt_ref, ga_ref, gb_ref, *, tm):
    x = x_ref[...]
    xb, xm, xl = _split3(x)
    w = A_HEADS * A_DK

    na = wacc_ref.shape[1] // 3
    t1 = _mm(xb, wacc_ref[...])
    t2 = _mm(xm, wacc_ref[:, :2 * na])
    t3 = _mm(xl, wacc_ref[:, :na])
    acc = _sum6(t1[:, :na], t1[:, na:2 * na], t2[:, :na], t1[:, 2 * na:], t3, t2[:, na:])
    qia_ref[...] = acc[:, :w]
    kia_ref[...] = acc[:, w:]
    wh, wmid, wlo = wiwh_ref[...], wiwm_ref[...], wiwl_ref[...]
    iwt_ref[...] = _sum6(_nt(wh, xb), _nt(wmid, xb), _nt(wh, xm), _nt(wlo, xb), _nt(wh, xl),
                         _nt(wmid, xm))

    def mm(j):
        return _mm(xb, wm_ref[:, j * w:(j + 1) * w])

    aq_ref[...] = mm(0)
    lb = lb_ref[...]
    f_ref[...] = lb + (1.0 - lb) * jax.nn.sigmoid(mm(1))
    ai_ref[...] = mm(2)
    ag = mm(3)
    gate_ref[...] = ag * jax.nn.sigmoid(ag)
    qb_ref[...] = (mm(4) * ATTN_SCALE).astype(BF16)
    kb_ref[...] = mm(5).astype(BF16)
    kvt = _nt(wkvt_ref[...], xb)
    kt_ref[0] = kvt[:w]
    vt = kvt[w:]
    vt_ref[0] = vt
    for r in range(tm // K_BLOCK):
        for p in range(B_HEADS // 2):
            vtb_ref[r, p] = vt[p * LANES:(p + 1) * LANES, r * K_BLOCK:(r + 1) * K_BLOCK].astype(BF16)
    kit_ref[0] = _nt(wikt_ref[...], xb)
    d = ga_ref.shape[1]
    ga_ref[...] = jax.nn.sigmoid(_mm(xb, wg_ref[:, :d]))
    gb_ref[...] = jax.nn.sigmoid(_mm(xb, wg_ref[:, d:]))


def _proj(x, wm, wkvt, wikt, wacc, wiw_h, wiw_m, wiw_l, wg, lb_row, *, batch, seq, tm):
    n, d = x.shape
    w = A_HEADS * A_DK
    nb = n // K_BLOCK
    tps = seq // tm
    row = lambda width: pl.BlockSpec((tm, width), lambda i: (i, 0))
    tmajor = lambda rows: pl.BlockSpec((1, rows, tm), lambda i: (i // tps, 0, i % tps))
    out_shape = (
        jax.ShapeDtypeStruct((n, w), F32),
        jax.ShapeDtypeStruct((n, w), F32),
        jax.ShapeDtypeStruct((n, w), F32),
        jax.ShapeDtypeStruct((n, w), F32),
        jax.ShapeDtypeStruct((n, w), BF16),
        jax.ShapeDtypeStruct((batch, w, seq), F32),
        jax.ShapeDtypeStruct((n, w), BF16),
        jax.ShapeDtypeStruct((batch, w, seq), F32),
        jax.ShapeDtypeStruct((nb, B_HEADS // 2, LANES, K_BLOCK), BF16),
        jax.ShapeDtypeStruct((batch, IDX_DIM, seq), F32),
        jax.ShapeDtypeStruct((n, w), F32),
        jax.ShapeDtypeStruct((n, 2 * IDX_DIM), F32),
        jax.ShapeDtypeStruct((IDX_HEADS, n), F32),
        jax.ShapeDtypeStruct((n, d), F32),
        jax.ShapeDtypeStruct((n, d), F32),
    )
    out_specs = (
        row(w), row(w), row(w), row(w), row(w), tmajor(w), row(w), tmajor(w),
        pl.BlockSpec((tm // K_BLOCK, B_HEADS // 2, LANES, K_BLOCK), lambda i: (i, 0, 0, 0)),
        tmajor(IDX_DIM), row(w), row(2 * IDX_DIM),
        pl.BlockSpec((IDX_HEADS, tm), lambda i: (0, i)),
        row(d), row(d),
    )
    consts = (wm, wkvt, wikt, wacc, wiw_h, wiw_m, wiw_l, wg, lb_row)
    return pl.pallas_call(
        functools.partial(_proj_kernel, tm=tm),
        grid=(n // tm,),
        in_specs=[row(d)] + [_const_spec(a.shape) for a in consts],
        out_specs=out_specs,
        out_shape=out_shape,
        compiler_params=_params("arbitrary"),
        name="proj",
    )(x, *consts)


def _hgrn_kernel(aq_ref, f_ref, ai_ref, gate_ref, gn_ref, o_ref, s_ref,
                 st_ref, kpad, gpad, vpad, dstack, rsum_s):
    C, BS = HGRN_CHUNK, HGRN_BLOCK
    c = pl.program_id(1)

    @pl.when(c == 0)
    def _():
        st_ref[...] = jnp.zeros_like(st_ref)
        kpad[...] = jnp.zeros_like(kpad)
        gpad[...] = jnp.zeros_like(gpad)
        vpad[...] = jnp.zeros_like(vpad)

    ri = lax.broadcasted_iota(I32, (C, C), 0)
    ci = lax.broadcasted_iota(I32, (C, C), 1)
    tri = jnp.where(ci <= ri, 1.0, 0.0).astype(BF16)

    row = lax.broadcasted_iota(I32, (C, A_DK), 0)
    blk = row // BS
    rr = ri // BS
    cc = ci // BS
    mask1 = (rr >= 2) & (cc < 2)
    mask2 = ((rr == 1) & (cc == 0)) | ((rr == 3) & (cc == 2))
    ones_bf = jnp.ones((A_DK, LANES), BF16)
    neg_inf = -jnp.inf

    for h in range(A_HEADS):
        sl = slice(h * A_DK, (h + 1) * A_DK)
        q = aq_ref[:, sl]
        f = f_ref[:, sl]
        v = ai_ref[:, sl]
        g = jnp.log(f)
        k = 1.0 - f
        g_hi = g.astype(BF16)
        r1 = g - g_hi.astype(F32)
        g_mid = r1.astype(BF16)
        g_lo = (r1 - g_mid.astype(F32)).astype(BF16)
        cs = _mm(tri, jnp.concatenate([g_hi, g_mid, g_lo], axis=1))
        G = cs[:, :A_DK] + cs[:, A_DK:2 * A_DK] + cs[:, 2 * A_DK:]
        g15 = G[BS - 1:BS]
        g31 = G[2 * BS - 1:2 * BS]
        g47 = G[3 * BS - 1:3 * BS]
        g_last = G[C - 1:C]

        v_bf = v.astype(BF16)
        st = st_ref[h]
        o = _nt((q * jnp.exp(G)).astype(BF16), st.astype(BF16))

        qe1 = jnp.exp(jnp.where(blk >= 2, G - g31, neg_inf))
        ke1 = jnp.exp(jnp.where(blk < 2, g31 - G, neg_inf))
        ref2 = jnp.where(blk < 2, g15, g47)
        qe2 = jnp.exp(jnp.where((blk == 1) | (blk == 3), G - ref2, neg_inf))
        ke2 = jnp.exp(jnp.where((blk == 0) | (blk == 2), ref2 - G, neg_inf))
        a1 = _nt((q * qe1).astype(BF16), (k * ke1).astype(BF16))
        a2 = _nt((q * qe2).astype(BF16), (k * ke2).astype(BF16))
        a = jnp.where(mask1, a1, 0.0) + jnp.where(mask2, a2, 0.0)
        o = o + _mm(a.astype(BF16), v_bf)

        kpad[BS:, :] = k
        gpad[BS:, :] = G
        vpad[BS:, :] = v
        for d in range(BS):
            ks = kpad[BS - d:BS - d + C, :]
            gs = gpad[BS - d:BS - d + C, :]
            e = jnp.exp(jnp.where((row % BS) >= d, G - gs, neg_inf))
            dstack[d * C:(d + 1) * C, :] = (q * ks * e).astype(BF16)
        rsum_s[...] = _mm(dstack[...], ones_bf)
        for d in range(BS):
            o = o + rsum_s[d * C:(d + 1) * C, :] * vpad[BS - d:BS - d + C, :]

        kt = (k * jnp.exp(g_last - G)).astype(BF16)
        st_new = st * jnp.exp(g_last) + _tn(v_bf, kt)
        st_ref[h] = st_new

        @pl.when(c == pl.num_programs(1) - 1)
        def _():
            s_ref[0, h] = st_new.T

        ms = jnp.mean(o * o, axis=-1, keepdims=True)
        on = o * lax.rsqrt(ms + RMS_EPS) * gn_ref[...]
        o_ref[:, sl] = (on * gate_ref[:, sl]).astype(BF16)


def _hgrn_prompt(aq, f, ai, gate, gn_row, *, batch, seq):
    n, w = aq.shape
    nc = seq // HGRN_CHUNK
    blk = pl.BlockSpec((HGRN_CHUNK, w), lambda b, c: (b * nc + c, 0))
    pad_rows = HGRN_BLOCK + HGRN_CHUNK
    return pl.pallas_call(
        _hgrn_kernel,
        grid=(batch, nc),
        in_specs=[blk, blk, blk, blk, _const_spec(gn_row.shape)],
        out_specs=(blk, pl.BlockSpec((1, A_HEADS, A_DK, A_DV), lambda b, c: (b, 0, 0, 0))),
        out_shape=(jax.ShapeDtypeStruct((n, w), BF16),
                   jax.ShapeDtypeStruct((batch, A_HEADS, A_DK, A_DV), F32)),
        scratch_shapes=[pltpu.VMEM((A_HEADS, A_DV, A_DK), F32),
                        pltpu.VMEM((pad_rows, A_DK), F32),
                        pltpu.VMEM((pad_rows, A_DK), F32),
                        pltpu.VMEM((pad_rows, A_DV), F32),
                        pltpu.VMEM((HGRN_BLOCK * HGRN_CHUNK, A_DK), BF16),
                        pltpu.VMEM((HGRN_BLOCK * HGRN_CHUNK, LANES), F32)],
        compiler_params=_params("arbitrary", "arbitrary"),
        name="hgrn_prompt",
    )(aq, f, ai, gate, gn_row)


def _for_blocks(nch, body, init):
    def trip(c, carry):
        for u in range(KB_UNROLL):
            carry = body(KB_UNROLL * c + u, carry)
        return carry
    return lax.fori_loop(0, nch, trip, init)


def _count_rows(key_s, nch, hits, n_out=1):
    def body(kb, accs):
        off = pl.multiple_of(kb * K_BLOCK, K_BLOCK)
        hs = hits(key_s[pl.ds(off, K_BLOCK), :], kb)
        return tuple(a + h.reshape(K_BLOCK // SUBLANES, SUBLANES, Q_BLOCK).sum(axis=0)
                     for a, h in zip(accs, hs))
    accs = _for_blocks(nch, body, tuple(jnp.zeros((SUBLANES, Q_BLOCK), I32) for _ in range(n_out)))
    return tuple(a.sum(axis=0, keepdims=True) for a in accs)


def _swap_halves(x):
    return jnp.concatenate([x[:, B_HDIM:], x[:, :B_HDIM]], axis=1)


def _attn_kernel(q_ref, qia_ref, iwt_ref, k_ref, kia_ref, vt_ref, o_ref,
                 kaug, kst1, kst2, r1_s, r2_s, rs_s, key_s, bias_s, lg_s, ot_s, *, seq, topk):
    i = pl.program_id(1)
    nch = (i + KB_UNROLL) // KB_UNROLL
    npair = B_HEADS // 2
    lane = lax.broadcasted_iota(I32, (Q_BLOCK, LANES), 1)
    krow = lax.broadcasted_iota(I32, (K_BLOCK, Q_BLOCK), 0)
    qpos = i * Q_BLOCK + lax.broadcasted_iota(I32, (K_BLOCK, Q_BLOCK), 1)

    @pl.when(i == 0)
    def _():
        def body(kb, carry):
            off = pl.multiple_of(kb * K_BLOCK, K_BLOCK)
            pos = off + lax.broadcasted_iota(I32, (K_BLOCK, LANES), 0)
            ln = lax.broadcasted_iota(I32, (K_BLOCK, LANES), 1)
            feat = jnp.where(ln == 0, (pos // 64) * 64, jnp.where(ln == 1, pos % 64, 0))
            feat = feat.astype(F32).astype(BF16)
            for p in range(npair):
                kaug[p, pl.ds(off, K_BLOCK), :LANES] = k_ref[pl.ds(off, K_BLOCK), p * LANES:(p + 1) * LANES]
                kaug[p, pl.ds(off, K_BLOCK), LANES:] = feat
            kh, km, kl = _split3(kia_ref[pl.ds(off, K_BLOCK), :])
            half = ln < IDX_DIM
            kst1[pl.ds(off, K_BLOCK), :LANES] = kh
            kst1[pl.ds(off, K_BLOCK), LANES:] = jnp.where(half, km, kh)
            kst2[pl.ds(off, K_BLOCK), :] = jnp.where(half, kl, km)
            return carry
        lax.fori_loop(0, seq // K_BLOCK, body, 0)

    zero_bf = jnp.zeros((Q_BLOCK, LANES), BF16)
    for p in range(npair):
        qp = q_ref[:, p * LANES:(p + 1) * LANES]
        lo = lane < B_HDIM
        qh, qm, ql = _split3(qia_ref[:, p * LANES:(p + 1) * LANES])
        rqh = _swap_halves(qh)
        a0 = jnp.where(lo, qh, _swap_halves(qm))
        b0 = jnp.where(lo, rqh, qm)
        r1_s[p, :Q_BLOCK, :LANES] = a0
        r1_s[p, :Q_BLOCK, LANES:] = jnp.where(lo, qh, _swap_halves(ql))
        r1_s[p, Q_BLOCK:, :LANES] = b0
        r1_s[p, Q_BLOCK:, LANES:] = jnp.where(lo, rqh, ql)
        r2_s[p, :Q_BLOCK, :] = a0
        r2_s[p, Q_BLOCK:, :] = b0
        rs_s[p, :Q_BLOCK, :LANES] = jnp.where(lo, qp, zero_bf)
        rs_s[p, Q_BLOCK:, :LANES] = jnp.where(lo, zero_bf, qp)
        m0 = 2.0 ** -(2 * p + 1)
        m1 = 2.0 ** -(2 * p + 2)
        rs_s[p, :Q_BLOCK, LANES:] = jnp.where(lane < 2, m0, 0.0).astype(BF16)
        rs_s[p, Q_BLOCK:, LANES:] = jnp.where(lane < 2, m1, 0.0).astype(BF16)

    def score_body(kb, carry):
        off = pl.multiple_of(kb * K_BLOCK, K_BLOCK)
        k1 = kst1[pl.ds(off, K_BLOCK), :]
        k2 = kst2[pl.ds(off, K_BLOCK), :]
        acc = jnp.zeros((K_BLOCK, Q_BLOCK), F32)
        for p in range(npair):
            s2 = _nt(k1, r1_s[p]) + _nt(k2, r2_s[p])
            for hh in range(2):
                hd = 2 * p + hh
                acc = acc + iwt_ref[hd:hd + 1, :] * jnp.maximum(s2[:, hh * Q_BLOCK:(hh + 1) * Q_BLOCK], 0.0)
        sc = jnp.where(off + krow <= qpos, acc * IDX_SCALE, -jnp.inf)
        key_s[pl.ds(off, K_BLOCK), :] = _float_key(sc)
        return carry
    _for_blocks(nch, score_body, 0)

    def bit_body(it, ans):
        cand = ans | lax.shift_left(jnp.int32(1), 31 - it)
        cnt, = _count_rows(key_s, nch, lambda blk, kb: (jnp.where(blk >= (cand ^ INT_MIN), 1, 0),))
        return jnp.where(cnt >= topk, cand, ans)
    ans = lax.fori_loop(0, 32, bit_body, jnp.zeros((1, Q_BLOCK), I32))
    thr = ans ^ INT_MIN
    thr_gt = jnp.maximum(thr, NEG_INF_KEY)
    n_gt, n_eq = _count_rows(
        key_s, nch, lambda blk, kb: (jnp.where(blk > thr_gt, 1, 0), jnp.where(blk == thr, 1, 0)), n_out=2)
    need = topk - n_gt

    def tie_search():
        def tie_body(it, y):
            cand = y | lax.shift_left(jnp.int32(1), 11 - it)
            cnt, = _count_rows(
                key_s, nch,
                lambda blk, kb: (jnp.where(blk == thr, jnp.where(kb * K_BLOCK + krow < cand, 1, 0), 0),))
            return jnp.where(cnt < need, cand, y)
        return lax.fori_loop(0, 12, tie_body, jnp.zeros((1, Q_BLOCK), I32))
    surplus = jnp.max(jnp.where(thr > NEG_INF_KEY, n_eq - need, 0))
    y = lax.cond(surplus > 0, tie_search, lambda: jnp.full((1, Q_BLOCK), 4095, I32))
    ylim = jnp.where(thr <= NEG_INF_KEY, 0, y + 1)

    def bias_body(kb, carry):
        off = pl.multiple_of(kb * K_BLOCK, K_BLOCK)
        blk = key_s[pl.ds(off, K_BLOCK), :]
        tie = jnp.where(blk == thr, jnp.where(off + krow < ylim, 0.0, -jnp.inf), -jnp.inf)
        bias_s[pl.ds(off, K_BLOCK), :] = jnp.where(blk > thr_gt, 0.0, tie)
        return carry
    _for_blocks(nch, bias_body, 0)

    def fold(x, op):
        return op(x.reshape(K_BLOCK // SUBLANES, SUBLANES, 2 * Q_BLOCK), axis=0)

    def pass1(kb, mxs):
        off = pl.multiple_of(kb * K_BLOCK, K_BLOCK)
        b = bias_s[pl.ds(off, K_BLOCK), :]
        b2 = jnp.concatenate([b, b], axis=1)
        out = []
        for p in range(npair):
            lg = _nt(kaug[p, pl.ds(off, K_BLOCK), :], rs_s[p]) + b2
            lg_s[p, pl.ds(off, K_BLOCK), :] = lg
            out.append(jnp.maximum(mxs[p], fold(lg, jnp.max)))
        return tuple(out)
    mxs = _for_blocks(nch, pass1, tuple(jnp.full((SUBLANES, 2 * Q_BLOCK), -jnp.inf, F32)
                                        for _ in range(npair)))
    ms = [mx.max(axis=0, keepdims=True) for mx in mxs]

    ot_s[...] = jnp.zeros_like(ot_s)

    def pass2(kb, l8s):
        off = pl.multiple_of(kb * K_BLOCK, K_BLOCK)
        out = []
        for p in range(npair):
            pe = jnp.exp(lg_s[p, pl.ds(off, K_BLOCK), :] - ms[p])
            ot_s[p] += _mm(vt_ref[kb, p], pe.astype(BF16))
            out.append(l8s[p] + fold(pe, jnp.sum))
        return tuple(out)
    l8s = _for_blocks(nch, pass2, tuple(jnp.zeros((SUBLANES, 2 * Q_BLOCK), F32) for _ in range(npair)))
    for p in range(npair):
        inv = 1.0 / l8s[p].sum(axis=0, keepdims=True)
        ot = ot_s[p]
        top = (ot[:B_HDIM, :Q_BLOCK] * inv[:, :Q_BLOCK]).T
        bot = (ot[B_HDIM:, Q_BLOCK:] * inv[:, Q_BLOCK:]).T
        o_ref[:, p * LANES:(p + 1) * LANES] = jnp.concatenate([top, bot], axis=1).astype(BF16)


def _attn_prompt(qb, qia, iwt, kb, kia, vtb, *, batch, seq, topk):
    n, w = qb.shape
    nq = seq // Q_BLOCK
    npair = B_HEADS // 2
    assert (seq // K_BLOCK) % KB_UNROLL == 0
    qspec = pl.BlockSpec((Q_BLOCK, w), lambda b, i: (b * nq + i, 0))
    return pl.pallas_call(
        functools.partial(_attn_kernel, seq=seq, topk=topk),
        grid=(batch, nq),
        in_specs=[qspec, qspec,
                  pl.BlockSpec((IDX_HEADS, Q_BLOCK), lambda b, i: (0, b * nq + i)),
                  pl.BlockSpec((seq, w), lambda b, i: (b, 0)),
                  pl.BlockSpec((seq, 2 * IDX_DIM), lambda b, i: (b, 0)),
                  pl.BlockSpec((seq // K_BLOCK, npair, LANES, K_BLOCK), lambda b, i: (b, 0, 0, 0))],
        out_specs=qspec,
        out_shape=jax.ShapeDtypeStruct((n, w), BF16),
        scratch_shapes=[pltpu.VMEM((npair, seq, 2 * LANES), BF16),
                        pltpu.VMEM((seq, 2 * LANES), BF16),
                        pltpu.VMEM((seq, LANES), BF16),
                        pltpu.VMEM((npair, 2 * Q_BLOCK, 2 * LANES), BF16),
                        pltpu.VMEM((npair, 2 * Q_BLOCK, LANES), BF16),
                        pltpu.VMEM((npair, 2 * Q_BLOCK, 2 * LANES), BF16),
                        pltpu.VMEM((seq, Q_BLOCK), I32),
                        pltpu.VMEM((seq, Q_BLOCK), F32),
                        pltpu.VMEM((npair, seq, 2 * Q_BLOCK), F32),
                        pltpu.VMEM((npair, LANES, 2 * Q_BLOCK), F32)],
        compiler_params=_params("arbitrary", "arbitrary"),
        name="attn_prompt",
    )(qb, qia, iwt, kb, kia, vtb)


def _layer_norm(x, g, b):
    mu = jnp.mean(x, axis=-1, keepdims=True)
    xc = x - mu
    var = jnp.mean(xc * xc, axis=-1, keepdims=True)
    return xc * lax.rsqrt(var + LN_EPS) * g + b


def _final_kernel(x_ref, oa_ref, ob_ref, ga_ref, gb_ref, wua_ref, wub_ref, wo_ref,
                  l1g_ref, l1b_ref, wfg_ref, wfu_ref, wfd_ref, l2g_ref, l2b_ref, y_ref,
                  *, alpha, ff_chunk):
    ya = _mm(oa_ref[...], wua_ref[...])
    yb = _mm(ob_ref[...], wub_ref[...])
    merged = (ga_ref[...] * ya + gb_ref[...] * yb).astype(BF16)
    mix = _mm(merged, wo_ref[...])
    x1 = _layer_norm(alpha * x_ref[...] + mix, l1g_ref[...], l1b_ref[...])
    x1b = x1.astype(BF16)
    d_ff = wfg_ref.shape[1]
    acc = jnp.zeros(x1.shape, F32)
    for c in range(d_ff // ff_chunk):
        cs = slice(c * ff_chunk, (c + 1) * ff_chunk)
        hg = _mm(x1b, wfg_ref[:, cs])
        hu = _mm(x1b, wfu_ref[:, cs])
        hc = (hg * jax.nn.sigmoid(hg) * hu).astype(BF16)
        acc = acc + _mm(hc, wfd_ref[cs, :])
    y_ref[...] = _layer_norm(alpha * x1 + acc, l2g_ref[...], l2b_ref[...])


def _final(x, oa, ob, ga, gb, wua, wub, wo, l1g, l1b, wfg, wfu, wfd, l2g, l2b, *, tm, alpha):
    n, d = x.shape
    w = oa.shape[1]
    row = lambda width: pl.BlockSpec((tm, width), lambda i: (i, 0))
    consts = (wua, wub, wo, l1g, l1b, wfg, wfu, wfd, l2g, l2b)
    return pl.pallas_call(
        functools.partial(_final_kernel, alpha=alpha, ff_chunk=256),
        grid=(n // tm,),
        in_specs=[row(d), row(w), row(w), row(d), row(d)] + [_const_spec(a.shape) for a in consts],
        out_specs=row(d),
        out_shape=jax.ShapeDtypeStruct((n, d), F32),
        compiler_params=_params("arbitrary"),
        name="merge_ffn",
    )(x, oa, ob, ga, gb, *consts)


def _hgrn_step_kernel(qt_ref, ft_ref, v_ref, gate_ref, gn_ref, s_ref, so_ref, o_ref):
    b = pl.program_id(0)
    qcol = _lane_column(qt_ref[...], b)
    fcol = _lane_column(ft_ref[...], b)
    for h in range(A_HEADS):
        sl = slice(h * A_DK, (h + 1) * A_DK)
        fc = fcol[sl]
        vrow = v_ref[0, :, sl]
        s_new = fc * s_ref[0, h] + (1.0 - fc) * vrow
        so_ref[0, h] = s_new
        o = jnp.sum(qcol[sl] * s_new, axis=0, keepdims=True)
        ms = jnp.mean(o * o, axis=-1, keepdims=True)
        on = o * lax.rsqrt(ms + RMS_EPS) * gn_ref[...]
        o_ref[0, :, sl] = (on * gate_ref[0, :, sl]).astype(BF16)


def _hgrn_step(aq_t, f_t, ai3, gate3, gn_row, state):
    nb = state.shape[0]
    w = A_HEADS * A_DK
    sspec = pl.BlockSpec((1, A_HEADS, A_DK, A_DV), lambda b: (b, 0, 0, 0))
    rspec = pl.BlockSpec((1, 1, w), lambda b: (b, 0, 0))
    return pl.pallas_call(
        _hgrn_step_kernel,
        grid=(nb,),
        in_specs=[_const_spec(aq_t.shape), _const_spec(f_t.shape), rspec, rspec,
                  _const_spec(gn_row.shape), sspec],
        out_specs=(sspec, rspec),
        out_shape=(jax.ShapeDtypeStruct(state.shape, F32),
                   jax.ShapeDtypeStruct((nb, 1, w), BF16)),
        compiler_params=_params("arbitrary"),
        name="hgrn_step",
    )(aq_t, f_t, ai3, gate3, gn_row, state)


def _page_copies(pt_ref, cache_ref, buf, sem, b, slot, n_pages, layer):
    def body(pg, carry):
        pltpu.make_async_copy(cache_ref.at[layer, pt_ref[b, pg]], buf.at[slot, pg], sem.at[slot]).start()
        return carry
    lax.fori_loop(0, n_pages, body, 0)


def _idx_score_kernel(pt_ref, qi_ref, iwt_ref, kn_ref, cache_ref, sc_ref, sn_ref, buf, sem,
                      *, n_pages, layer):
    b = pl.program_id(0)
    nb = pl.num_programs(0)
    slot = b % 2

    @pl.when(b == 0)
    def _():
        _page_copies(pt_ref, cache_ref, buf, sem, b, slot, n_pages, layer)

    @pl.when(b + 1 < nb)
    def _():
        _page_copies(pt_ref, cache_ref, buf, sem, b + 1, 1 - slot, n_pages, layer)

    qi = qi_ref[0]
    wcol = _lane_column(iwt_ref[...], b)

    def head_sum(s):
        return jnp.sum(wcol * jnp.maximum(s, 0.0), axis=0, keepdims=True) * IDX_SCALE

    sn = jnp.sum(qi * kn_ref[0][:, :IDX_DIM], axis=1, keepdims=True)
    sn_ref[0] = jnp.broadcast_to(head_sum(sn), (1, LANES))

    qh, qm, ql = _split3(qi)
    lhs1 = jnp.concatenate([qh, qm, qh, ql], axis=1)
    lhs2 = jnp.concatenate([qh, qm], axis=1)

    def page_scores(tile):
        kh, km, kl = _split3(tile)
        s = _mm(lhs1, jnp.concatenate([kh, kh, km, kh], axis=0))
        return head_sum(s + _mm(lhs2, jnp.concatenate([kl, km], axis=0)))

    pltpu.make_async_copy(buf.at[slot], buf.at[slot], sem.at[slot]).wait()
    for g in range(n_pages // SUBLANES):
        rows = [page_scores(buf[slot, g * SUBLANES + r]) for r in range(SUBLANES)]
        sc_ref[0, g * SUBLANES:(g + 1) * SUBLANES, :] = jnp.concatenate(rows, axis=0)


def _idx_scores(page_table, qi3, iwt, kidx_new3, cache_kidx_t, *, layer):
    nb, n_pages = page_table.shape
    assert n_pages % SUBLANES == 0
    grid_spec = pltpu.PrefetchScalarGridSpec(
        num_scalar_prefetch=1,
        grid=(nb,),
        in_specs=[pl.BlockSpec((1, IDX_HEADS, IDX_DIM), lambda b, pt: (b, 0, 0)),
                  _const_spec(iwt.shape),
                  pl.BlockSpec((1, 1, 2 * IDX_DIM), lambda b, pt: (b, 0, 0)),
                  pl.BlockSpec(memory_space=pl.ANY)],
        out_specs=(pl.BlockSpec((1, n_pages, PAGE_SIZE), lambda b, pt: (b, 0, 0)),
                   pl.BlockSpec((1, 1, LANES), lambda b, pt: (b, 0, 0))),
        scratch_shapes=[pltpu.VMEM((2, n_pages, IDX_DIM, PAGE_SIZE), F32),
                        pltpu.SemaphoreType.DMA((2,))],
    )
    return pl.pallas_call(
        functools.partial(_idx_score_kernel, n_pages=n_pages, layer=layer),
        grid_spec=grid_spec,
        out_shape=(jax.ShapeDtypeStruct((nb, n_pages, PAGE_SIZE), F32),
                   jax.ShapeDtypeStruct((nb, 1, LANES), F32)),
        compiler_params=_params("arbitrary"),
        name="idx_scores",
    )(page_table, qi3, iwt, kidx_new3, cache_kidx_t)


def _select_kernel(sc_ref, sn_ref, mask_ref, newsel_ref, key_s, *, topk, pos_bits):
    nb, past = sc_ref.shape
    ncol = past // LANES
    key_s[...] = _float_key(sc_ref[...])
    kn = _float_key(sn_ref[...])
    pos = lax.broadcasted_iota(I32, (nb, LANES), 1)

    def count(hits_past, hit_new):
        def body(j, acc):
            off = pl.multiple_of(j * LANES, LANES)
            return acc + hits_past(key_s[:, pl.ds(off, LANES)], off)
        acc = lax.fori_loop(0, ncol, body, jnp.zeros((nb, LANES), I32))
        tot = jnp.sum(acc, axis=1, keepdims=True)
        return jnp.broadcast_to(tot, (nb, LANES)) + hit_new

    def bit_body(it, ans):
        cand = ans | lax.shift_left(jnp.int32(1), 31 - it)
        c = cand ^ INT_MIN
        cnt = count(lambda blk, off: jnp.where(blk >= c, 1, 0), jnp.where(kn >= c, 1, 0))
        return jnp.where(cnt >= topk, cand, ans)
    ans = lax.fori_loop(0, 32, bit_body, jnp.zeros((nb, LANES), I32))
    thr = ans ^ INT_MIN
    thr_gt = jnp.maximum(thr, NEG_INF_KEY)
    need = topk - count(lambda blk, off: jnp.where(blk > thr_gt, 1, 0), jnp.where(kn > thr_gt, 1, 0))

    def tie_body(it, y):
        cand = y | lax.shift_left(jnp.int32(1), pos_bits - 1 - it)
        cnt = count(lambda blk, off: jnp.where(blk == thr, jnp.where(off + pos < cand, 1, 0), 0),
                    jnp.where(kn == thr, jnp.where(past < cand, 1, 0), 0))
        return jnp.where(cnt < need, cand, y)
    y = lax.fori_loop(0, pos_bits, tie_body, jnp.zeros((nb, LANES), I32))
    ylim = jnp.where(thr <= NEG_INF_KEY, 0, y + 1)

    def out_body(j, carry):
        off = pl.multiple_of(j * LANES, LANES)
        blk = key_s[:, pl.ds(off, LANES)]
        tie = jnp.where(blk == thr, jnp.where(off + pos < ylim, 1.0, 0.0), 0.0)
        mask_ref[:, pl.ds(off, LANES)] = jnp.where(blk > thr_gt, 1.0, tie)
        return carry
    lax.fori_loop(0, ncol, out_body, 0)
    tie_new = jnp.where(kn == thr, jnp.where(past < ylim, 1, 0), 0)
    newsel_ref[...] = jnp.where(kn > thr_gt, 1, tie_new)


def _select(scores, snew, *, topk):
    nb, past = scores.shape
    pos_bits = (past + 1).bit_length()
    return pl.pallas_call(
        functools.partial(_select_kernel, topk=topk, pos_bits=pos_bits),
        out_shape=(jax.ShapeDtypeStruct((nb, past), F32),
                   jax.ShapeDtypeStruct((nb, LANES), I32)),
        scratch_shapes=[pltpu.VMEM((nb, past), I32)],
        compiler_params=pltpu.CompilerParams(vmem_limit_bytes=VMEM_LIMIT_BYTES),
        name="topk_select",
    )(scores, snew)


def _paged_attn_kernel(pt_ref, qt_ref, knt_ref, vnt_ref, mask_ref, nsel_ref, ck_ref, cv_ref, ot_ref,
                       kring, vring, sem, qcol_s, p_s, acc_s, *, n_pages, ring, layer):
    b = pl.program_id(0)
    nb = pl.num_programs(0)
    past = n_pages * PAGE_SIZE

    def page_copy(cache_ref, buf, which, bb, pg):
        return pltpu.make_async_copy(cache_ref.at[layer, pt_ref[bb, pg]], buf.at[pg % ring],
                                     sem.at[which, pg % ring])

    def start_ahead(cache_ref, buf, which, pg):
        n = pg + ring
        wrap = n >= n_pages
        bb = jnp.where(wrap, b + 1, b)
        pg2 = jnp.where(wrap, n - n_pages, n)

        @pl.when(bb < nb)
        def _():
            page_copy(cache_ref, buf, which, bb, pg2).start()

    @pl.when(b == 0)
    def _():
        ot_ref[...] = jnp.zeros_like(ot_ref)
        for r in range(ring):
            page_copy(ck_ref, kring, 0, 0, r).start()
            page_copy(cv_ref, vring, 1, 0, r).start()

    qcol = _lane_column(qt_ref[...], b)
    kncol = _lane_column(knt_ref[...], b)
    vncol = _lane_column(vnt_ref[...], b)
    for h in range(B_HEADS):
        qcol_s[h] = jnp.broadcast_to(qcol[h * B_HDIM:(h + 1) * B_HDIM], (B_HDIM, PAGE_SIZE))

    sub = lax.broadcasted_iota(I32, (B_HEADS, PAGE_SIZE), 0)
    lane = lax.broadcasted_iota(I32, (B_HEADS, PAGE_SIZE), 1)
    slopes = jnp.exp2(-(sub + 1).astype(F32))

    def head_rows(tile_of_head):
        out = jnp.zeros((B_HEADS, PAGE_SIZE), F32)
        for h in range(B_HEADS):
            out = jnp.where(sub == h, jnp.sum(tile_of_head(h), axis=0, keepdims=True), out)
        return out

    def k_body(pg, mx):
        page_copy(ck_ref, kring, 0, b, pg).wait()
        lg = head_rows(lambda h: kring[pg % ring, h] * qcol_s[h])
        dist = (past - pg * PAGE_SIZE - lane).astype(F32)
        lg = jnp.where(mask_ref[0, pl.ds(pg, 1), :] > 0.0, lg - slopes * dist, -jnp.inf)
        p_s[pg] = lg
        start_ahead(ck_ref, kring, 0, pg)
        return jnp.maximum(mx, lg)
    mx = lax.fori_loop(0, n_pages, k_body, jnp.full((B_HEADS, PAGE_SIZE), -jnp.inf, F32))

    lgn = head_rows(lambda h: qcol_s[h] * kncol[h * B_HDIM:(h + 1) * B_HDIM])
    lgn = jnp.where(nsel_ref[pl.ds(b, 1), :] > 0, lgn, -jnp.inf)
    m = jnp.maximum(jnp.max(mx, axis=1, keepdims=True), lgn)
    pn = jnp.exp(lgn - m)

    def e_body(pg, l):
        pe = jnp.exp(p_s[pg] - m)
        p_s[pg] = pe
        return l + pe
    lsum = lax.fori_loop(0, n_pages, e_body, jnp.zeros((B_HEADS, PAGE_SIZE), F32))
    inv = 1.0 / (jnp.sum(lsum, axis=1, keepdims=True) + pn)

    acc_s[...] = jnp.zeros_like(acc_s)

    def v_body(pg, carry):
        page_copy(cv_ref, vring, 1, b, pg).wait()
        pe = p_s[pg]
        for h in range(B_HEADS):
            acc_s[h] += vring[pg % ring, h] * pe[h:h + 1, :]
        start_ahead(cv_ref, vring, 1, pg)
        return carry
    lax.fori_loop(0, n_pages, v_body, 0)

    lane_o = lax.broadcasted_iota(I32, (B_HDIM, ot_ref.shape[1]), 1)
    for h in range(B_HEADS):
        hs = slice(h * B_HDIM, (h + 1) * B_HDIM)
        oc = jnp.sum(acc_s[h], axis=1, keepdims=True) + pn[h:h + 1, :1] * vncol[hs]
        oc = oc * inv[h:h + 1, :1]
        ot_ref[hs, :] = jnp.where(lane_o == b, oc, ot_ref[hs, :])


def _paged_attn(page_table, q_t, kn_t, vn_t, mask3, newsel, cache_k_t, cache_v_t, *, layer):
    nb, n_pages = page_table.shape
    ring = min(PAGE_RING, n_pages)
    assert n_pages % ring == 0
    w = B_HEADS * B_HDIM
    page = (B_HEADS, B_HDIM, PAGE_SIZE)
    grid_spec = pltpu.PrefetchScalarGridSpec(
        num_scalar_prefetch=1,
        grid=(nb,),
        in_specs=[_const_spec(q_t.shape), _const_spec(kn_t.shape), _const_spec(vn_t.shape),
                  pl.BlockSpec((1, n_pages, PAGE_SIZE), lambda b, pt: (b, 0, 0)),
                  _const_spec(newsel.shape),
                  pl.BlockSpec(memory_space=pl.ANY),
                  pl.BlockSpec(memory_space=pl.ANY)],
        out_specs=pl.BlockSpec((w, nb), lambda b, pt: (0, 0)),
        scratch_shapes=[pltpu.VMEM((ring,) + page, F32),
                        pltpu.VMEM((ring,) + page, F32),
                        pltpu.SemaphoreType.DMA((2, ring)),
                        pltpu.VMEM(page, F32),
                        pltpu.VMEM((n_pages, B_HEADS, PAGE_SIZE), F32),
                        pltpu.VMEM(page, F32)],
    )
    return pl.pallas_call(
        functools.partial(_paged_attn_kernel, n_pages=n_pages, ring=ring, layer=layer),
        grid_spec=grid_spec,
        out_shape=jax.ShapeDtypeStruct((w, nb), F32),
        compiler_params=_params("arbitrary"),
        name="paged_attn",
    )(page_table, q_t, kn_t, vn_t, mask3, newsel, cache_k_t, cache_v_t)


def _prep_weights(lb, w_in, hgrn_norm_g, w_up_a, w_up_b, w_o, ln1_g, ln1_b,
                  w_ffn_gate, w_ffn_up, w_ffn_down, ln2_g, ln2_b):
    w = A_HEADS * A_DK
    o_k, o_v, o_iq = 5 * w, 6 * w, 7 * w
    o_ik = 8 * w
    o_iw = o_ik + IDX_DIM
    o_g = o_iw + IDX_HEADS
    wb = w_in.astype(BF16)
    wt = w_in.T.astype(BF16)
    row = lambda a: a.reshape(1, -1).astype(F32)

    def split3(a):
        hi = a.astype(BF16)
        r = a - hi.astype(F32)
        mid = r.astype(BF16)
        return hi, mid, (r - mid.astype(F32)).astype(BF16)

    w_idx = jnp.concatenate([w_in[:, o_iq:o_ik], w_in[:, o_ik:o_iw], w_in[:, o_ik:o_iw]], axis=1)
    wiw_h, wiw_m, wiw_l = split3(w_in[:, o_iw:o_g].T)
    return dict(
        wm=wb[:, :o_v],
        wkvt=wt[o_k:o_v + w],
        wikt=wt[o_ik:o_iw],
        wacc=jnp.concatenate(split3(w_idx), axis=1),
        wiw_h=wiw_h, wiw_m=wiw_m, wiw_l=wiw_l,
        wg=wb[:, o_g:],
        lb=row(lb),
        gn=row(hgrn_norm_g),
        wua=w_up_a.astype(BF16), wub=w_up_b.astype(BF16), wo=w_o.astype(BF16),
        l1g=row(ln1_g), l1b=row(ln1_b),
        wfg=w_ffn_gate.astype(BF16), wfu=w_ffn_up.astype(BF16), wfd=w_ffn_down.astype(BF16),
        l2g=row(ln2_g), l2b=row(ln2_b),
    )


def _proj_call(x2, pw, *, batch, seq, tm):
    return _proj(x2, pw["wm"], pw["wkvt"], pw["wikt"], pw["wacc"], pw["wiw_h"], pw["wiw_m"], pw["wiw_l"],
                 pw["wg"], pw["lb"], batch=batch, seq=seq, tm=tm)


def _final_call(x2, oa, ob, ga, gb, pw, *, tm, alpha):
    return _final(x2, oa, ob, ga, gb, pw["wua"], pw["wub"], pw["wo"], pw["l1g"], pw["l1b"],
                  pw["wfg"], pw["wfu"], pw["wfd"], pw["l2g"], pw["l2b"], tm=tm, alpha=alpha)


def _token_major(feat_t, batch, seq, heads):
    if heads is None:
        return feat_t.transpose(0, 2, 1)
    return feat_t.reshape(batch, heads, -1, seq).transpose(0, 3, 1, 2)


def _prompt_layer(x, pw, *, alpha):
    batch, seq, d = x.shape
    n = batch * seq
    x2 = x.reshape(n, d)
    (aq, f, ai, gate, qb, kt, kb, vt, vtb, kit, qia, kia, iwt, ga, gb) = _proj_call(
        x2, pw, batch=batch, seq=seq, tm=256)
    oa, s_new = _hgrn_prompt(aq, f, ai, gate, pw["gn"], batch=batch, seq=seq)
    ob = _attn_prompt(qb, qia, iwt, kb, kia, vtb, batch=batch, seq=seq, topk=min(TOPK_MAX, seq // 4))
    y = _final_call(x2, oa, ob, ga, gb, pw, tm=512, alpha=alpha)
    return (y.reshape(batch, seq, d), s_new,
            _token_major(kt, batch, seq, B_HEADS), _token_major(vt, batch, seq, B_HEADS),
            _token_major(kit, batch, seq, None))


def _sample_layer(x, state, cache_k_t, cache_v_t, cache_kidx_t, page_table, pw, *, alpha, layer):
    nb, tn, d = x.shape
    n_pages = page_table.shape[1]
    past = n_pages * PAGE_SIZE
    topk = min(TOPK_MAX, (past + tn) // 4)
    w = A_HEADS * A_DK
    x2 = x.reshape(nb, d)
    (aq, f, ai, gate, qb, kt, kb, vt, vtb, kit, qia, kia, iwt, ga, gb) = _proj_call(
        x2, pw, batch=1, seq=nb, tm=nb)
    s_new, oa3 = _hgrn_step(aq.T, f.T, ai.reshape(nb, 1, w), gate.reshape(nb, 1, w), pw["gn"], state)
    scores, snew = _idx_scores(page_table, qia.reshape(nb, IDX_HEADS, IDX_DIM), iwt,
                               kia.reshape(nb, 1, 2 * IDX_DIM), cache_kidx_t, layer=layer)
    mask, newsel = _select(scores.reshape(nb, past), snew.reshape(nb, LANES), topk=topk)
    ob_t = _paged_attn(page_table, qb.astype(F32).T, kt[0], vt[0], mask.reshape(nb, n_pages, PAGE_SIZE),
                       newsel, cache_k_t, cache_v_t, layer=layer)
    y = _final_call(x2, oa3.reshape(nb, w), ob_t.T.astype(BF16), ga, gb, pw, tm=nb, alpha=alpha)
    tok = lambda a, heads: _token_major(a, 1, nb, heads)[0][:, None]
    return (y.reshape(nb, tn, d), s_new, tok(kt, B_HEADS), tok(vt, B_HEADS), tok(kit, None))


def kernel(x_prompt, x_sample, cache_k, cache_v, cache_kidx, state_hgrn, page_table, hgrn_lb_logits,
           w_in, hgrn_norm_g, w_up_a, w_up_b, w_o, ln1_g, ln1_b, w_ffn_gate, w_ffn_up, w_ffn_down,
           ln2_g, ln2_b):
    depth = w_in.shape[0]
    alpha = (2.0 * depth) ** 0.25
    lb_all = jnp.cumsum(jax.nn.softmax(hgrn_lb_logits.astype(F32), axis=0), axis=0)
    cache_k_t = cache_k.transpose(0, 1, 3, 4, 2)
    cache_v_t = cache_v.transpose(0, 1, 3, 4, 2)
    cache_kidx_t = cache_kidx.transpose(0, 1, 3, 2)
    xp, xs = x_prompt, x_sample
    outs_p, outs_s = [], []
    for l in range(depth):
        pw = _prep_weights(lb_all[l], w_in[l], hgrn_norm_g[l], w_up_a[l], w_up_b[l], w_o[l],
                           ln1_g[l], ln1_b[l], w_ffn_gate[l], w_ffn_up[l], w_ffn_down[l],
                           ln2_g[l], ln2_b[l])
        xp, sp, kp, vp, kip = _prompt_layer(xp, pw, alpha=alpha)
        xs, ss, ks, vs, kis = _sample_layer(xs, state_hgrn[l], cache_k_t, cache_v_t, cache_kidx_t,
                                            page_table, pw, alpha=alpha, layer=l)
        outs_p.append((kp, vp, kip, sp))
        outs_s.append((ks, vs, kis, ss))
    stack = lambda outs, j: jnp.stack([o[j] for o in outs], 0)
    return (xp, xs, stack(outs_p, 0), stack(outs_p, 1), stack(outs_p, 2), stack(outs_p, 3),
            stack(outs_s, 0), stack(outs_s, 1), stack(outs_s, 2), stack(outs_s, 3))
```

```python
import functools

import jax
import jax.numpy as jnp
from jax import lax
from jax.experimental import pallas as pl
from jax.experimental.pallas import tpu as pltpu

F32 = jnp.float32
BF16 = jnp.bfloat16
I32 = jnp.int32

A_HEADS = 4
A_DK = 128
A_DV = 128
B_HEADS = 8
B_HDIM = 64
IDX_HEADS = 8
IDX_DIM = 64
TOPK_MAX = 256
PAGE_SIZE = 128
LN_EPS = 1e-5
RMS_EPS = 1e-6
ATTN_SCALE = B_HDIM ** -0.5
IDX_SCALE = (IDX_DIM * IDX_HEADS) ** -0.5

LANES = 128
SUBLANES = 8
VMEM_LIMIT_BYTES = 56 * 1024 * 1024

HGRN_CHUNK = 64
HGRN_BLOCK = 16
Q_BLOCK = 128
K_BLOCK = 128
K_TILE = 256
PAGE_RING = 32

INT_MIN = -2 ** 31
NEG_INF_KEY = INT_MIN + 0x007FFFFF


def _nt(a, b):
    return lax.dot_general(a, b, (((1,), (1,)), ((), ())), preferred_element_type=F32)


def _tn(a, b):
    return lax.dot_general(a, b, (((0,), (0,)), ((), ())), preferred_element_type=F32)


def _mm(a, b):
    return jnp.dot(a, b, preferred_element_type=F32)


def _const_spec(shape):
    nd = len(shape)
    return pl.BlockSpec(shape, lambda *_: (0,) * nd, pipeline_mode=pl.Buffered(1))


def _params(*sem):
    return pltpu.CompilerParams(dimension_semantics=sem, vmem_limit_bytes=VMEM_LIMIT_BYTES)


def _float_key(x):
    bits = pltpu.bitcast(x, I32)
    return bits ^ ((bits >> 31) & 0x7FFFFFFF)


def _lane_column(x, b):
    lane = lax.broadcasted_iota(I32, x.shape, 1)
    return jnp.sum(jnp.where(lane == b, x, 0.0), axis=1, keepdims=True)


def _split3(x):
    hi = x.astype(BF16)
    r = x - hi.astype(F32)
    mid = r.astype(BF16)
    lo = (r - mid.astype(F32)).astype(BF16)
    return hi, mid, lo


def _sum6(hh, hm, mh, hl, lh, mm):
    return hh + ((hm + mh) + ((hl + lh) + mm))


def _proj_kernel(x_ref, wm_ref, wkvt_ref, wikt_ref, wacc_ref, wiwh_ref, wiwm_ref, wiwl_ref, wg_ref, lb_ref,
                 aq_ref, f_ref, ai_ref, gate_ref, qb_ref, kt_ref, kb_ref, vt_ref, vtb_ref,
                 kit_ref, qia_ref, kia_ref, iwt_ref, ga_ref, gb_ref, *, tm):
    x = x_ref[...]
    xb, xm, xl = _split3(x)
    w = A_HEADS * A_DK

    na = wacc_ref.shape[1] // 3
    t1 = _mm(xb, wacc_ref[...])
    t2 = _mm(xm, wacc_ref[:, :2 * na])
    t3 = _mm(xl, wacc_ref[:, :na])
    acc = _sum6(t1[:, :na], t1[:, na:2 * na], t2[:, :na], t1[:, 2 * na:], t3, t2[:, na:])
    qia_ref[...] = acc[:, :w]
    kia_ref[...] = acc[:, w:]
    wh, wmid, wlo = wiwh_ref[...], wiwm_ref[...], wiwl_ref[...]
    iwt_ref[...] = _sum6(_nt(wh, xb), _nt(wmid, xb), _nt(wh, xm), _nt(wlo, xb), _nt(wh, xl),
                         _nt(wmid, xm))

    def mm(j):
        return _mm(xb, wm_ref[:, j * w:(j + 1) * w])

    aq_ref[...] = mm(0)
    lb = lb_ref[...]
    f_ref[...] = lb + (1.0 - lb) * jax.nn.sigmoid(mm(1))
    ai_ref[...] = mm(2)
    ag = mm(3)
    gate_ref[...] = ag * jax.nn.sigmoid(ag)
    qb_ref[...] = (mm(4) * ATTN_SCALE).astype(BF16)
    kb_ref[...] = mm(5).astype(BF16)
    kvt = _nt(wkvt_ref[...], xb)
    kt_ref[0] = kvt[:w]
    vt = kvt[w:]
    vt_ref[0] = vt
    for r in range(tm // K_BLOCK):
        for p in range(B_HEADS // 2):
            vtb_ref[r, p] = vt[p * LANES:(p + 1) * LANES, r * K_BLOCK:(r + 1) * K_BLOCK].astype(BF16)
    kit_ref[0] = _nt(wikt_ref[...], xb)
    d = ga_ref.shape[1]
    ga_ref[...] = jax.nn.sigmoid(_mm(xb, wg_ref[:, :d]))
    gb_ref[...] = jax.nn.sigmoid(_mm(xb, wg_ref[:, d:]))


def _proj(x, wm, wkvt, wikt, wacc, wiw_h, wiw_m, wiw_l, wg, lb_row, *, batch, seq, tm):
    n, d = x.shape
    w = A_HEADS * A_DK
    nb = n // K_BLOCK
    tps = seq // tm
    row = lambda width: pl.BlockSpec((tm, width), lambda i: (i, 0))
    tmajor = lambda rows: pl.BlockSpec((1, rows, tm), lambda i: (i // tps, 0, i % tps))
    out_shape = (
        jax.ShapeDtypeStruct((n, w), F32),
        jax.ShapeDtypeStruct((n, w), F32),
        jax.ShapeDtypeStruct((n, w), F32),
        jax.ShapeDtypeStruct((n, w), F32),
        jax.ShapeDtypeStruct((n, w), BF16),
        jax.ShapeDtypeStruct((batch, w, seq), F32),
        jax.ShapeDtypeStruct((n, w), BF16),
        jax.ShapeDtypeStruct((batch, w, seq), F32),
        jax.ShapeDtypeStruct((nb, B_HEADS // 2, LANES, K_BLOCK), BF16),
        jax.ShapeDtypeStruct((batch, IDX_DIM, seq), F32),
        jax.ShapeDtypeStruct((n, w), F32),
        jax.ShapeDtypeStruct((n, 2 * IDX_DIM), F32),
        jax.ShapeDtypeStruct((IDX_HEADS, n), F32),
        jax.ShapeDtypeStruct((n, d), F32),
        jax.ShapeDtypeStruct((n, d), F32),
    )
    out_specs = (
        row(w), row(w), row(w), row(w), row(w), tmajor(w), row(w), tmajor(w),
        pl.BlockSpec((tm // K_BLOCK, B_HEADS // 2, LANES, K_BLOCK), lambda i: (i, 0, 0, 0)),
        tmajor(IDX_DIM), row(w), row(2 * IDX_DIM),
        pl.BlockSpec((IDX_HEADS, tm), lambda i: (0, i)),
        row(d), row(d),
    )
    consts = (wm, wkvt, wikt, wacc, wiw_h, wiw_m, wiw_l, wg, lb_row)
    return pl.pallas_call(
        functools.partial(_proj_kernel, tm=tm),
        grid=(n // tm,),
        in_specs=[row(d)] + [_const_spec(a.shape) for a in consts],
        out_specs=out_specs,
        out_shape=out_shape,
        compiler_params=_params("arbitrary"),
        name="proj",
    )(x, *consts)


def _hgrn_kernel(aq_ref, f_ref, ai_ref, gate_ref, gn_ref, o_ref, s_ref,
                 st_ref, *head_scratch):
    C, BS = HGRN_CHUNK, HGRN_BLOCK
    c = pl.program_id(1)
    per_head = [head_scratch[5 * h:5 * h + 5] for h in range(A_HEADS)]

    @pl.when(c == 0)
    def _():
        st_ref[...] = jnp.zeros_like(st_ref)
        for kpad, gpad, vpad, _, _ in per_head:
            kpad[...] = jnp.zeros_like(kpad)
            gpad[...] = jnp.zeros_like(gpad)
            vpad[...] = jnp.zeros_like(vpad)

    ri = lax.broadcasted_iota(I32, (C, C), 0)
    ci = lax.broadcasted_iota(I32, (C, C), 1)
    tri = jnp.where(ci <= ri, 1.0, 0.0).astype(BF16)

    row = lax.broadcasted_iota(I32, (C, A_DK), 0)
    blk = row // BS
    rr = ri // BS
    cc = ci // BS
    mask1 = (rr >= 2) & (cc < 2)
    mask2 = ((rr == 1) & (cc == 0)) | ((rr == 3) & (cc == 2))
    ones_bf = jnp.ones((A_DK, LANES), BF16)
    neg_inf = -jnp.inf

    for h in range(A_HEADS):
        sl = slice(h * A_DK, (h + 1) * A_DK)
        q = aq_ref[:, sl]
        f = f_ref[:, sl]
        v = ai_ref[:, sl]
        g = jnp.log(f)
        k = 1.0 - f
        g_hi = g.astype(BF16)
        r1 = g - g_hi.astype(F32)
        g_mid = r1.astype(BF16)
        g_lo = (r1 - g_mid.astype(F32)).astype(BF16)
        cs = _mm(tri, jnp.concatenate([g_hi, g_mid, g_lo], axis=1))
        G = cs[:, :A_DK] + cs[:, A_DK:2 * A_DK] + cs[:, 2 * A_DK:]
        g15 = G[BS - 1:BS]
        g31 = G[2 * BS - 1:2 * BS]
        g47 = G[3 * BS - 1:3 * BS]
        g_last = G[C - 1:C]

        v_bf = v.astype(BF16)
        st = st_ref[h]
        o = _nt((q * jnp.exp(G)).astype(BF16), st.astype(BF16))

        qe1 = jnp.exp(jnp.where(blk >= 2, G - g31, neg_inf))
        ke1 = jnp.exp(jnp.where(blk < 2, g31 - G, neg_inf))
        ref2 = jnp.where(blk < 2, g15, g47)
        qe2 = jnp.exp(jnp.where((blk == 1) | (blk == 3), G - ref2, neg_inf))
        ke2 = jnp.exp(jnp.where((blk == 0) | (blk == 2), ref2 - G, neg_inf))
        a1 = _nt((q * qe1).astype(BF16), (k * ke1).astype(BF16))
        a2 = _nt((q * qe2).astype(BF16), (k * ke2).astype(BF16))
        a = jnp.where(mask1, a1, 0.0) + jnp.where(mask2, a2, 0.0)
        o = o + _mm(a.astype(BF16), v_bf)

        kpad, gpad, vpad, dstack, rsum_s = per_head[h]
        kpad[BS:, :] = k
        gpad[BS:, :] = G
        vpad[BS:, :] = v
        for d in range(BS):
            ks = kpad[BS - d:BS - d + C, :]
            gs = gpad[BS - d:BS - d + C, :]
            e = jnp.exp(jnp.where((row % BS) >= d, G - gs, neg_inf))
            dstack[d * C:(d + 1) * C, :] = (q * ks * e).astype(BF16)
        rsum_s[...] = _mm(dstack[...], ones_bf)
        for d in range(BS):
            o = o + rsum_s[d * C:(d + 1) * C, :] * vpad[BS - d:BS - d + C, :]

        kt = (k * jnp.exp(g_last - G)).astype(BF16)
        st_new = st * jnp.exp(g_last) + _tn(v_bf, kt)
        st_ref[h] = st_new

        ms = jnp.mean(o * o, axis=-1, keepdims=True)
        on = o * lax.rsqrt(ms + RMS_EPS) * gn_ref[...]
        o_ref[:, sl] = (on * gate_ref[:, sl]).astype(BF16)

    @pl.when(c == pl.num_programs(1) - 1)
    def _():
        for h in range(A_HEADS):
            s_ref[0, h] = st_ref[h].T


def _hgrn_prompt(aq, f, ai, gate, gn_row, *, batch, seq):
    n, w = aq.shape
    nc = seq // HGRN_CHUNK
    blk = pl.BlockSpec((HGRN_CHUNK, w), lambda b, c: (b * nc + c, 0))
    pad_rows = HGRN_BLOCK + HGRN_CHUNK
    return pl.pallas_call(
        _hgrn_kernel,
        grid=(batch, nc),
        in_specs=[blk, blk, blk, blk, _const_spec(gn_row.shape)],
        out_specs=(blk, pl.BlockSpec((1, A_HEADS, A_DK, A_DV), lambda b, c: (b, 0, 0, 0))),
        out_shape=(jax.ShapeDtypeStruct((n, w), BF16),
                   jax.ShapeDtypeStruct((batch, A_HEADS, A_DK, A_DV), F32)),
        scratch_shapes=[pltpu.VMEM((A_HEADS, A_DV, A_DK), F32)] + A_HEADS * [
            pltpu.VMEM((pad_rows, A_DK), F32),
            pltpu.VMEM((pad_rows, A_DK), F32),
            pltpu.VMEM((pad_rows, A_DV), F32),
            pltpu.VMEM((HGRN_BLOCK * HGRN_CHUNK, A_DK), BF16),
            pltpu.VMEM((HGRN_BLOCK * HGRN_CHUNK, LANES), F32)],
        compiler_params=_params("arbitrary", "arbitrary"),
        name="hgrn_prompt",
    )(aq, f, ai, gate, gn_row)


def _count_rows(key_s, nkt, hits, n_out=1):
    def body(kt, accs):
        off = pl.multiple_of(kt * K_TILE, K_TILE)
        hs = hits(key_s[pl.ds(off, K_TILE), :], kt)
        return tuple(a + h.reshape(K_TILE // SUBLANES, SUBLANES, Q_BLOCK).sum(axis=0)
                     for a, h in zip(accs, hs))
    accs = lax.fori_loop(0, nkt, body, tuple(jnp.zeros((SUBLANES, Q_BLOCK), I32) for _ in range(n_out)))
    return tuple(a.sum(axis=0, keepdims=True) for a in accs)


def _swap_halves(x):
    return jnp.concatenate([x[:, B_HDIM:], x[:, :B_HDIM]], axis=1)


def _attn_kernel(q_ref, qia_ref, iwt_ref, k_ref, kia_ref, vt_ref, o_ref,
                 kaug, kst1, kst2, r1_s, r2_s, rs_s, key_s, bias_s, lg_s, ot_s, *, seq, topk):
    i = pl.program_id(1)
    nkt = (i * Q_BLOCK) // K_TILE + 1
    npair = B_HEADS // 2
    lane = lax.broadcasted_iota(I32, (Q_BLOCK, LANES), 1)
    krow = lax.broadcasted_iota(I32, (K_TILE, Q_BLOCK), 0)
    qpos = i * Q_BLOCK + lax.broadcasted_iota(I32, (K_TILE, Q_BLOCK), 1)

    @pl.when(i == 0)
    def _():
        def body(kb, carry):
            off = pl.multiple_of(kb * K_BLOCK, K_BLOCK)
            pos = off + lax.broadcasted_iota(I32, (K_BLOCK, LANES), 0)
            ln = lax.broadcasted_iota(I32, (K_BLOCK, LANES), 1)
            feat = jnp.where(ln == 0, (pos // 64) * 64, jnp.where(ln == 1, pos % 64, 0))
            feat = feat.astype(F32).astype(BF16)
            for p in range(npair):
                kaug[p, pl.ds(off, K_BLOCK), :LANES] = k_ref[pl.ds(off, K_BLOCK), p * LANES:(p + 1) * LANES]
                kaug[p, pl.ds(off, K_BLOCK), LANES:] = feat
            kh, km, kl = _split3(kia_ref[pl.ds(off, K_BLOCK), :])
            half = ln < IDX_DIM
            kst1[pl.ds(off, K_BLOCK), :LANES] = kh
            kst1[pl.ds(off, K_BLOCK), LANES:] = jnp.where(half, km, kh)
            kst2[pl.ds(off, K_BLOCK), :] = jnp.where(half, kl, km)
            return carry
        lax.fori_loop(0, seq // K_BLOCK, body, 0)

    zero_bf = jnp.zeros((Q_BLOCK, LANES), BF16)
    for p in range(npair):
        qp = q_ref[:, p * LANES:(p + 1) * LANES]
        lo = lane < B_HDIM
        qh, qm, ql = _split3(qia_ref[:, p * LANES:(p + 1) * LANES])
        rqh = _swap_halves(qh)
        a0 = jnp.where(lo, qh, _swap_halves(qm))
        b0 = jnp.where(lo, rqh, qm)
        r1_s[p, :Q_BLOCK, :LANES] = a0
        r1_s[p, :Q_BLOCK, LANES:] = jnp.where(lo, qh, _swap_halves(ql))
        r1_s[p, Q_BLOCK:, :LANES] = b0
        r1_s[p, Q_BLOCK:, LANES:] = jnp.where(lo, rqh, ql)
        r2_s[p, :Q_BLOCK, :] = a0
        r2_s[p, Q_BLOCK:, :] = b0
        rs_s[p, :Q_BLOCK, :LANES] = jnp.where(lo, qp, zero_bf)
        rs_s[p, Q_BLOCK:, :LANES] = jnp.where(lo, zero_bf, qp)
        m0 = 2.0 ** -(2 * p + 1)
        m1 = 2.0 ** -(2 * p + 2)
        rs_s[p, :Q_BLOCK, LANES:] = jnp.where(lane < 2, m0, 0.0).astype(BF16)
        rs_s[p, Q_BLOCK:, LANES:] = jnp.where(lane < 2, m1, 0.0).astype(BF16)

    def score_body(kt, carry):
        off = pl.multiple_of(kt * K_TILE, K_TILE)
        k1 = kst1[pl.ds(off, K_TILE), :]
        k2 = kst2[pl.ds(off, K_TILE), :]
        acc = jnp.zeros((K_TILE, Q_BLOCK), F32)
        for p in range(npair):
            s2 = _nt(k1, r1_s[p]) + _nt(k2, r2_s[p])
            for hh in range(2):
                hd = 2 * p + hh
                acc = acc + iwt_ref[hd:hd + 1, :] * jnp.maximum(s2[:, hh * Q_BLOCK:(hh + 1) * Q_BLOCK], 0.0)
        sc = jnp.where(off + krow <= qpos, acc * IDX_SCALE, -jnp.inf)
        key_s[pl.ds(off, K_TILE), :] = _float_key(sc)
        return carry
    lax.fori_loop(0, nkt, score_body, 0)

    def bit_body(it, ans):
        cand = ans | lax.shift_left(jnp.int32(1), 31 - it)
        cnt, = _count_rows(key_s, nkt, lambda blk, kt: (jnp.where(blk >= (cand ^ INT_MIN), 1, 0),))
        return jnp.where(cnt >= topk, cand, ans)
    ans = lax.fori_loop(0, 32, bit_body, jnp.zeros((1, Q_BLOCK), I32))
    thr = ans ^ INT_MIN
    thr_gt = jnp.maximum(thr, NEG_INF_KEY)
    n_gt, n_eq = _count_rows(
        key_s, nkt, lambda blk, kt: (jnp.where(blk > thr_gt, 1, 0), jnp.where(blk == thr, 1, 0)), n_out=2)
    need = topk - n_gt

    def tie_search():
        def tie_body(it, y):
            cand = y | lax.shift_left(jnp.int32(1), 11 - it)
            cnt, = _count_rows(
                key_s, nkt,
                lambda blk, kt: (jnp.where(blk == thr, jnp.where(kt * K_TILE + krow < cand, 1, 0), 0),))
            return jnp.where(cnt < need, cand, y)
        return lax.fori_loop(0, 12, tie_body, jnp.zeros((1, Q_BLOCK), I32))
    surplus = jnp.max(jnp.where(thr > NEG_INF_KEY, n_eq - need, 0))
    y = lax.cond(surplus > 0, tie_search, lambda: jnp.full((1, Q_BLOCK), 4095, I32))
    ylim = jnp.where(thr <= NEG_INF_KEY, 0, y + 1)

    def bias_body(kt, carry):
        off = pl.multiple_of(kt * K_TILE, K_TILE)
        blk = key_s[pl.ds(off, K_TILE), :]
        tie = jnp.where(blk == thr, jnp.where(off + krow < ylim, 0.0, -jnp.inf), -jnp.inf)
        bias_s[pl.ds(off, K_TILE), :] = jnp.where(blk > thr_gt, 0.0, tie)
        return carry
    lax.fori_loop(0, nkt, bias_body, 0)

    def fold(x, op):
        return op(x.reshape(x.shape[0] // SUBLANES, SUBLANES, 2 * Q_BLOCK), axis=0)

    def pass1(kt, mxs):
        off = pl.multiple_of(kt * K_TILE, K_TILE)
        b = bias_s[pl.ds(off, K_TILE), :]
        b2 = jnp.concatenate([b, b], axis=1)
        out = []
        for p in range(npair):
            lg = _nt(kaug[p, pl.ds(off, K_TILE), :], rs_s[p]) + b2
            lg_s[p, pl.ds(off, K_TILE), :] = lg
            out.append(jnp.maximum(mxs[p], fold(lg, jnp.max)))
        return tuple(out)
    mxs = lax.fori_loop(0, nkt, pass1, tuple(jnp.full((SUBLANES, 2 * Q_BLOCK), -jnp.inf, F32)
                                             for _ in range(npair)))
    ms = [mx.max(axis=0, keepdims=True) for mx in mxs]

    ot_s[...] = jnp.zeros_like(ot_s)

    def pass2(kt, l8s):
        out = list(l8s)
        for u in range(K_TILE // K_BLOCK):
            kb = kt * (K_TILE // K_BLOCK) + u
            off = pl.multiple_of(kb * K_BLOCK, K_BLOCK)
            for p in range(npair):
                pe = jnp.exp(lg_s[p, pl.ds(off, K_BLOCK), :] - ms[p])
                ot_s[p] += _mm(vt_ref[kb, p], pe.astype(BF16))
                out[p] = out[p] + fold(pe, jnp.sum)
        return tuple(out)
    l8s = lax.fori_loop(0, nkt, pass2, tuple(jnp.zeros((SUBLANES, 2 * Q_BLOCK), F32) for _ in range(npair)))
    for p in range(npair):
        inv = 1.0 / l8s[p].sum(axis=0, keepdims=True)
        ot = ot_s[p]
        top = (ot[:B_HDIM, :Q_BLOCK] * inv[:, :Q_BLOCK]).T
        bot = (ot[B_HDIM:, Q_BLOCK:] * inv[:, Q_BLOCK:]).T
        o_ref[:, p * LANES:(p + 1) * LANES] = jnp.concatenate([top, bot], axis=1).astype(BF16)


def _attn_prompt(qb, qia, iwt, kb, kia, vtb, *, batch, seq, topk):
    n, w = qb.shape
    nq = seq // Q_BLOCK
    npair = B_HEADS // 2
    assert seq % K_TILE == 0 and K_TILE % K_BLOCK == 0
    qspec = pl.BlockSpec((Q_BLOCK, w), lambda b, i: (b * nq + i, 0))
    return pl.pallas_call(
        functools.partial(_attn_kernel, seq=seq, topk=topk),
        grid=(batch, nq),
        in_specs=[qspec, qspec,
                  pl.BlockSpec((IDX_HEADS, Q_BLOCK), lambda b, i: (0, b * nq + i)),
                  pl.BlockSpec((seq, w), lambda b, i: (b, 0)),
                  pl.BlockSpec((seq, 2 * IDX_DIM), lambda b, i: (b, 0)),
                  pl.BlockSpec((seq // K_BLOCK, npair, LANES, K_BLOCK), lambda b, i: (b, 0, 0, 0))],
        out_specs=qspec,
        out_shape=jax.ShapeDtypeStruct((n, w), BF16),
        scratch_shapes=[pltpu.VMEM((npair, seq, 2 * LANES), BF16),
                        pltpu.VMEM((seq, 2 * LANES), BF16),
                        pltpu.VMEM((seq, LANES), BF16),
                        pltpu.VMEM((npair, 2 * Q_BLOCK, 2 * LANES), BF16),
                        pltpu.VMEM((npair, 2 * Q_BLOCK, LANES), BF16),
                        pltpu.VMEM((npair, 2 * Q_BLOCK, 2 * LANES), BF16),
                        pltpu.VMEM((seq, Q_BLOCK), I32),
                        pltpu.VMEM((seq, Q_BLOCK), F32),
                        pltpu.VMEM((npair, seq, 2 * Q_BLOCK), F32),
                        pltpu.VMEM((npair, LANES, 2 * Q_BLOCK), F32)],
        compiler_params=_params("arbitrary", "arbitrary"),
        name="attn_prompt",
    )(qb, qia, iwt, kb, kia, vtb)


def _layer_norm(x, g, b):
    mu = jnp.mean(x, axis=-1, keepdims=True)
    xc = x - mu
    var = jnp.mean(xc * xc, axis=-1, keepdims=True)
    return xc * lax.rsqrt(var + LN_EPS) * g + b


def _final_kernel(x_ref, oa_ref, ob_ref, ga_ref, gb_ref, wua_ref, wub_ref, wo_ref,
                  l1g_ref, l1b_ref, wfg_ref, wfu_ref, wfd_ref, l2g_ref, l2b_ref, y_ref,
                  *, alpha, ff_chunk):
    ya = _mm(oa_ref[...], wua_ref[...])
    yb = _mm(ob_ref[...], wub_ref[...])
    merged = (ga_ref[...] * ya + gb_ref[...] * yb).astype(BF16)
    mix = _mm(merged, wo_ref[...])
    x1 = _layer_norm(alpha * x_ref[...] + mix, l1g_ref[...], l1b_ref[...])
    x1b = x1.astype(BF16)
    d_ff = wfg_ref.shape[1]
    acc = jnp.zeros(x1.shape, F32)
    for c in range(d_ff // ff_chunk):
        cs = slice(c * ff_chunk, (c + 1) * ff_chunk)
        hg = _mm(x1b, wfg_ref[:, cs])
        hu = _mm(x1b, wfu_ref[:, cs])
        hc = (hg * jax.nn.sigmoid(hg) * hu).astype(BF16)
        acc = acc + _mm(hc, wfd_ref[cs, :])
    y_ref[...] = _layer_norm(alpha * x1 + acc, l2g_ref[...], l2b_ref[...])


def _final(x, oa, ob, ga, gb, wua, wub, wo, l1g, l1b, wfg, wfu, wfd, l2g, l2b, *, tm, alpha):
    n, d = x.shape
    w = oa.shape[1]
    row = lambda width: pl.BlockSpec((tm, width), lambda i: (i, 0))
    consts = (wua, wub, wo, l1g, l1b, wfg, wfu, wfd, l2g, l2b)
    return pl.pallas_call(
        functools.partial(_final_kernel, alpha=alpha, ff_chunk=256),
        grid=(n // tm,),
        in_specs=[row(d), row(w), row(w), row(d), row(d)] + [_const_spec(a.shape) for a in consts],
        out_specs=row(d),
        out_shape=jax.ShapeDtypeStruct((n, d), F32),
        compiler_params=_params("arbitrary"),
        name="merge_ffn",
    )(x, oa, ob, ga, gb, *consts)


def _hgrn_step_kernel(qt_ref, ft_ref, v_ref, gate_ref, gn_ref, s_ref, so_ref, o_ref):
    b = pl.program_id(0)
    qcol = _lane_column(qt_ref[...], b)
    fcol = _lane_column(ft_ref[...], b)
    for h in range(A_HEADS):
        sl = slice(h * A_DK, (h + 1) * A_DK)
        fc = fcol[sl]
        vrow = v_ref[0, :, sl]
        s_new = fc * s_ref[0, h] + (1.0 - fc) * vrow
        so_ref[0, h] = s_new
        o = jnp.sum(qcol[sl] * s_new, axis=0, keepdims=True)
        ms = jnp.mean(o * o, axis=-1, keepdims=True)
        on = o * lax.rsqrt(ms + RMS_EPS) * gn_ref[...]
        o_ref[0, :, sl] = (on * gate_ref[0, :, sl]).astype(BF16)


def _hgrn_step(aq_t, f_t, ai3, gate3, gn_row, state):
    nb = state.shape[0]
    w = A_HEADS * A_DK
    sspec = pl.BlockSpec((1, A_HEADS, A_DK, A_DV), lambda b: (b, 0, 0, 0))
    rspec = pl.BlockSpec((1, 1, w), lambda b: (b, 0, 0))
    return pl.pallas_call(
        _hgrn_step_kernel,
        grid=(nb,),
        in_specs=[_const_spec(aq_t.shape), _const_spec(f_t.shape), rspec, rspec,
                  _const_spec(gn_row.shape), sspec],
        out_specs=(sspec, rspec),
        out_shape=(jax.ShapeDtypeStruct(state.shape, F32),
                   jax.ShapeDtypeStruct((nb, 1, w), BF16)),
        compiler_params=_params("arbitrary"),
        name="hgrn_step",
    )(aq_t, f_t, ai3, gate3, gn_row, state)


def _page_copies(pt_ref, cache_ref, buf, sem, b, slot, n_pages, layer):
    def body(pg, carry):
        pltpu.make_async_copy(cache_ref.at[layer, pt_ref[b, pg]], buf.at[slot, pg], sem.at[slot]).start()
        return carry
    lax.fori_loop(0, n_pages, body, 0)


def _idx_score_kernel(pt_ref, qi_ref, iwt_ref, kn_ref, cache_ref, sc_ref, sn_ref, buf, sem,
                      *, n_pages, layer):
    b = pl.program_id(0)
    nb = pl.num_programs(0)
    slot = b % 2

    @pl.when(b == 0)
    def _():
        _page_copies(pt_ref, cache_ref, buf, sem, b, slot, n_pages, layer)

    @pl.when(b + 1 < nb)
    def _():
        _page_copies(pt_ref, cache_ref, buf, sem, b + 1, 1 - slot, n_pages, layer)

    qi = qi_ref[0]
    wcol = _lane_column(iwt_ref[...], b)

    def head_sum(s):
        return jnp.sum(wcol * jnp.maximum(s, 0.0), axis=0, keepdims=True) * IDX_SCALE

    sn = jnp.sum(qi * kn_ref[0][:, :IDX_DIM], axis=1, keepdims=True)
    sn_ref[0] = jnp.broadcast_to(head_sum(sn), (1, LANES))

    qh, qm, ql = _split3(qi)
    lhs1 = jnp.concatenate([qh, qm, qh, ql], axis=1)
    lhs2 = jnp.concatenate([qh, qm], axis=1)

    def page_scores(tile):
        kh, km, kl = _split3(tile)
        s = _mm(lhs1, jnp.concatenate([kh, kh, km, kh], axis=0))
        return head_sum(s + _mm(lhs2, jnp.concatenate([kl, km], axis=0)))

    pltpu.make_async_copy(buf.at[slot], buf.at[slot], sem.at[slot]).wait()
    for g in range(n_pages // SUBLANES):
        rows = [page_scores(buf[slot, g * SUBLANES + r]) for r in range(SUBLANES)]
        sc_ref[0, g * SUBLANES:(g + 1) * SUBLANES, :] = jnp.concatenate(rows, axis=0)


def _idx_scores(page_table, qi3, iwt, kidx_new3, cache_kidx_t, *, layer):
    nb, n_pages = page_table.shape
    assert n_pages % SUBLANES == 0
    grid_spec = pltpu.PrefetchScalarGridSpec(
        num_scalar_prefetch=1,
        grid=(nb,),
        in_specs=[pl.BlockSpec((1, IDX_HEADS, IDX_DIM), lambda b, pt: (b, 0, 0)),
                  _const_spec(iwt.shape),
                  pl.BlockSpec((1, 1, 2 * IDX_DIM), lambda b, pt: (b, 0, 0)),
                  pl.BlockSpec(memory_space=pl.ANY)],
        out_specs=(pl.BlockSpec((1, n_pages, PAGE_SIZE), lambda b, pt: (b, 0, 0)),
                   pl.BlockSpec((1, 1, LANES), lambda b, pt: (b, 0, 0))),
        scratch_shapes=[pltpu.VMEM((2, n_pages, IDX_DIM, PAGE_SIZE), F32),
                        pltpu.SemaphoreType.DMA((2,))],
    )
    return pl.pallas_call(
        functools.partial(_idx_score_kernel, n_pages=n_pages, layer=layer),
        grid_spec=grid_spec,
        out_shape=(jax.ShapeDtypeStruct((nb, n_pages, PAGE_SIZE), F32),
                   jax.ShapeDtypeStruct((nb, 1, LANES), F32)),
        compiler_params=_params("arbitrary"),
        name="idx_scores",
    )(page_table, qi3, iwt, kidx_new3, cache_kidx_t)


def _select_kernel(sc_ref, sn_ref, mask_ref, newsel_ref, key_s, *, topk, pos_bits):
    nb, past = sc_ref.shape
    ncol = past // LANES
    key_s[...] = _float_key(sc_ref[...])
    kn = _float_key(sn_ref[...])
    pos = lax.broadcasted_iota(I32, (nb, LANES), 1)

    def count(hits_past, hit_new):
        def body(j, acc):
            off = pl.multiple_of(j * LANES, LANES)
            return acc + hits_past(key_s[:, pl.ds(off, LANES)], off)
        acc = lax.fori_loop(0, ncol, body, jnp.zeros((nb, LANES), I32))
        tot = jnp.sum(acc, axis=1, keepdims=True)
        return jnp.broadcast_to(tot, (nb, LANES)) + hit_new

    def bit_body(it, ans):
        cand = ans | lax.shift_left(jnp.int32(1), 31 - it)
        c = cand ^ INT_MIN
        cnt = count(lambda blk, off: jnp.where(blk >= c, 1, 0), jnp.where(kn >= c, 1, 0))
        return jnp.where(cnt >= topk, cand, ans)
    ans = lax.fori_loop(0, 32, bit_body, jnp.zeros((nb, LANES), I32))
    thr = ans ^ INT_MIN
    thr_gt = jnp.maximum(thr, NEG_INF_KEY)
    need = topk - count(lambda blk, off: jnp.where(blk > thr_gt, 1, 0), jnp.where(kn > thr_gt, 1, 0))

    def tie_body(it, y):
        cand = y | lax.shift_left(jnp.int32(1), pos_bits - 1 - it)
        cnt = count(lambda blk, off: jnp.where(blk == thr, jnp.where(off + pos < cand, 1, 0), 0),
                    jnp.where(kn == thr, jnp.where(past < cand, 1, 0), 0))
        return jnp.where(cnt < need, cand, y)
    y = lax.fori_loop(0, pos_bits, tie_body, jnp.zeros((nb, LANES), I32))
    ylim = jnp.where(thr <= NEG_INF_KEY, 0, y + 1)

    def out_body(j, carry):
        off = pl.multiple_of(j * LANES, LANES)
        blk = key_s[:, pl.ds(off, LANES)]
        tie = jnp.where(blk == thr, jnp.where(off + pos < ylim, 1.0, 0.0), 0.0)
        mask_ref[:, pl.ds(off, LANES)] = jnp.where(blk > thr_gt, 1.0, tie)
        return carry
    lax.fori_loop(0, ncol, out_body, 0)
    tie_new = jnp.where(kn == thr, jnp.where(past < ylim, 1, 0), 0)
    newsel_ref[...] = jnp.where(kn > thr_gt, 1, tie_new)


def _select(scores, snew, *, topk):
    nb, past = scores.shape
    pos_bits = (past + 1).bit_length()
    return pl.pallas_call(
        functools.partial(_select_kernel, topk=topk, pos_bits=pos_bits),
        out_shape=(jax.ShapeDtypeStruct((nb, past), F32),
                   jax.ShapeDtypeStruct((nb, LANES), I32)),
        scratch_shapes=[pltpu.VMEM((nb, past), I32)],
        compiler_params=pltpu.CompilerParams(vmem_limit_bytes=VMEM_LIMIT_BYTES),
        name="topk_select",
    )(scores, snew)


def _paged_attn_kernel(pt_ref, qt_ref, knt_ref, vnt_ref, mask_ref, nsel_ref, ck_ref, cv_ref, ot_ref,
                       kring, vring, sem, qcol_s, p_s, acc_s, *, n_pages, ring, layer):
    b = pl.program_id(0)
    nb = pl.num_programs(0)
    past = n_pages * PAGE_SIZE

    def page_copy(cache_ref, buf, which, bb, pg):
        return pltpu.make_async_copy(cache_ref.at[layer, pt_ref[bb, pg]], buf.at[pg % ring],
                                     sem.at[which, pg % ring])

    def start_ahead(cache_ref, buf, which, pg):
        n = pg + ring
        wrap = n >= n_pages
        bb = jnp.where(wrap, b + 1, b)
        pg2 = jnp.where(wrap, n - n_pages, n)

        @pl.when(bb < nb)
        def _():
            page_copy(cache_ref, buf, which, bb, pg2).start()

    @pl.when(b == 0)
    def _():
        ot_ref[...] = jnp.zeros_like(ot_ref)
        for r in range(ring):
            page_copy(ck_ref, kring, 0, 0, r).start()
            page_copy(cv_ref, vring, 1, 0, r).start()

    qcol = _lane_column(qt_ref[...], b)
    kncol = _lane_column(knt_ref[...], b)
    vncol = _lane_column(vnt_ref[...], b)
    for h in range(B_HEADS):
        qcol_s[h] = jnp.broadcast_to(qcol[h * B_HDIM:(h + 1) * B_HDIM], (B_HDIM, PAGE_SIZE))

    sub = lax.broadcasted_iota(I32, (B_HEADS, PAGE_SIZE), 0)
    lane = lax.broadcasted_iota(I32, (B_HEADS, PAGE_SIZE), 1)
    slopes = jnp.exp2(-(sub + 1).astype(F32))

    def head_rows(tile_of_head):
        out = jnp.zeros((B_HEADS, PAGE_SIZE), F32)
        for h in range(B_HEADS):
            out = jnp.where(sub == h, jnp.sum(tile_of_head(h), axis=0, keepdims=True), out)
        return out

    def k_body(pg, mx):
        page_copy(ck_ref, kring, 0, b, pg).wait()
        lg = head_rows(lambda h: kring[pg % ring, h] * qcol_s[h])
        dist = (past - pg * PAGE_SIZE - lane).astype(F32)
        lg = jnp.where(mask_ref[0, pl.ds(pg, 1), :] > 0.0, lg - slopes * dist, -jnp.inf)
        p_s[pg] = lg
        start_ahead(ck_ref, kring, 0, pg)
        return jnp.maximum(mx, lg)
    mx = lax.fori_loop(0, n_pages, k_body, jnp.full((B_HEADS, PAGE_SIZE), -jnp.inf, F32))

    lgn = head_rows(lambda h: qcol_s[h] * kncol[h * B_HDIM:(h + 1) * B_HDIM])
    lgn = jnp.where(nsel_ref[pl.ds(b, 1), :] > 0, lgn, -jnp.inf)
    m = jnp.maximum(jnp.max(mx, axis=1, keepdims=True), lgn)
    pn = jnp.exp(lgn - m)

    def e_body(pg, l):
        pe = jnp.exp(p_s[pg] - m)
        p_s[pg] = pe
        return l + pe
    lsum = lax.fori_loop(0, n_pages, e_body, jnp.zeros((B_HEADS, PAGE_SIZE), F32))
    inv = 1.0 / (jnp.sum(lsum, axis=1, keepdims=True) + pn)

    acc_s[...] = jnp.zeros_like(acc_s)

    def v_body(pg, carry):
        page_copy(cv_ref, vring, 1, b, pg).wait()
        pe = p_s[pg]
        for h in range(B_HEADS):
            acc_s[h] += vring[pg % ring, h] * pe[h:h + 1, :]
        start_ahead(cv_ref, vring, 1, pg)
        return carry
    lax.fori_loop(0, n_pages, v_body, 0)

    lane_o = lax.broadcasted_iota(I32, (B_HDIM, ot_ref.shape[1]), 1)
    for h in range(B_HEADS):
        hs = slice(h * B_HDIM, (h + 1) * B_HDIM)
        oc = jnp.sum(acc_s[h], axis=1, keepdims=True) + pn[h:h + 1, :1] * vncol[hs]
        oc = oc * inv[h:h + 1, :1]
        ot_ref[hs, :] = jnp.where(lane_o == b, oc, ot_ref[hs, :])


def _paged_attn(page_table, q_t, kn_t, vn_t, mask3, newsel, cache_k_t, cache_v_t, *, layer):
    nb, n_pages = page_table.shape
    ring = min(PAGE_RING, n_pages)
    assert n_pages % ring == 0
    w = B_HEADS * B_HDIM
    page = (B_HEADS, B_HDIM, PAGE_SIZE)
    grid_spec = pltpu.PrefetchScalarGridSpec(
        num_scalar_prefetch=1,
        grid=(nb,),
        in_specs=[_const_spec(q_t.shape), _const_spec(kn_t.shape), _const_spec(vn_t.shape),
                  pl.BlockSpec((1, n_pages, PAGE_SIZE), lambda b, pt: (b, 0, 0)),
                  _const_spec(newsel.shape),
                  pl.BlockSpec(memory_space=pl.ANY),
                  pl.BlockSpec(memory_space=pl.ANY)],
        out_specs=pl.BlockSpec((w, nb), lambda b, pt: (0, 0)),
        scratch_shapes=[pltpu.VMEM((ring,) + page, F32),
                        pltpu.VMEM((ring,) + page, F32),
                        pltpu.SemaphoreType.DMA((2, ring)),
                        pltpu.VMEM(page, F32),
                        pltpu.VMEM((n_pages, B_HEADS, PAGE_SIZE), F32),
                        pltpu.VMEM(page, F32)],
    )
    return pl.pallas_call(
        functools.partial(_paged_attn_kernel, n_pages=n_pages, ring=ring, layer=layer),
        grid_spec=grid_spec,
        out_shape=jax.ShapeDtypeStruct((w, nb), F32),
        compiler_params=_params("arbitrary"),
        name="paged_attn",
    )(page_table, q_t, kn_t, vn_t, mask3, newsel, cache_k_t, cache_v_t)


def _prep_weights(lb, w_in, hgrn_norm_g, w_up_a, w_up_b, w_o, ln1_g, ln1_b,
                  w_ffn_gate, w_ffn_up, w_ffn_down, ln2_g, ln2_b):
    w = A_HEADS * A_DK
    o_k, o_v, o_iq = 5 * w, 6 * w, 7 * w
    o_ik = 8 * w
    o_iw = o_ik + IDX_DIM
    o_g = o_iw + IDX_HEADS
    wb = w_in.astype(BF16)
    wt = w_in.T.astype(BF16)
    row = lambda a: a.reshape(1, -1).astype(F32)

    def split3(a):
        hi = a.astype(BF16)
        r = a - hi.astype(F32)
        mid = r.astype(BF16)
        return hi, mid, (r - mid.astype(F32)).astype(BF16)

    w_idx = jnp.concatenate([w_in[:, o_iq:o_ik], w_in[:, o_ik:o_iw], w_in[:, o_ik:o_iw]], axis=1)
    wiw_h, wiw_m, wiw_l = split3(w_in[:, o_iw:o_g].T)
    return dict(
        wm=wb[:, :o_v],
        wkvt=wt[o_k:o_v + w],
        wikt=wt[o_ik:o_iw],
        wacc=jnp.concatenate(split3(w_idx), axis=1),
        wiw_h=wiw_h, wiw_m=wiw_m, wiw_l=wiw_l,
        wg=wb[:, o_g:],
        lb=row(lb),
        gn=row(hgrn_norm_g),
        wua=w_up_a.astype(BF16), wub=w_up_b.astype(BF16), wo=w_o.astype(BF16),
        l1g=row(ln1_g), l1b=row(ln1_b),
        wfg=w_ffn_gate.astype(BF16), wfu=w_ffn_up.astype(BF16), wfd=w_ffn_down.astype(BF16),
        l2g=row(ln2_g), l2b=row(ln2_b),
    )


def _proj_call(x2, pw, *, batch, seq, tm):
    return _proj(x2, pw["wm"], pw["wkvt"], pw["wikt"], pw["wacc"], pw["wiw_h"], pw["wiw_m"], pw["wiw_l"],
                 pw["wg"], pw["lb"], batch=batch, seq=seq, tm=tm)


def _final_call(x2, oa, ob, ga, gb, pw, *, tm, alpha):
    return _final(x2, oa, ob, ga, gb, pw["wua"], pw["wub"], pw["wo"], pw["l1g"], pw["l1b"],
                  pw["wfg"], pw["wfu"], pw["wfd"], pw["l2g"], pw["l2b"], tm=tm, alpha=alpha)


def _token_major(feat_t, batch, seq, heads):
    if heads is None:
        return feat_t.transpose(0, 2, 1)
    return feat_t.reshape(batch, heads, -1, seq).transpose(0, 3, 1, 2)


def _prompt_layer(x, pw, *, alpha):
    batch, seq, d = x.shape
    n = batch * seq
    x2 = x.reshape(n, d)
    (aq, f, ai, gate, qb, kt, kb, vt, vtb, kit, qia, kia, iwt, ga, gb) = _proj_call(
        x2, pw, batch=batch, seq=seq, tm=256)
    oa, s_new = _hgrn_prompt(aq, f, ai, gate, pw["gn"], batch=batch, seq=seq)
    ob = _attn_prompt(qb, qia, iwt, kb, kia, vtb, batch=batch, seq=seq, topk=min(TOPK_MAX, seq // 4))
    y = _final_call(x2, oa, ob, ga, gb, pw, tm=512, alpha=alpha)
    return (y.reshape(batch, seq, d), s_new,
            _token_major(kt, batch, seq, B_HEADS), _token_major(vt, batch, seq, B_HEADS),
            _token_major(kit, batch, seq, None))


def _sample_layer(x, state, cache_k_t, cache_v_t, cache_kidx_t, page_table, pw, *, alpha, layer):
    nb, tn, d = x.shape
    n_pages = page_table.shape[1]
    past = n_pages * PAGE_SIZE
    topk = min(TOPK_MAX, (past + tn) // 4)
    w = A_HEADS * A_DK
    x2 = x.reshape(nb, d)
    (aq, f, ai, gate, qb, kt, kb, vt, vtb, kit, qia, kia, iwt, ga, gb) = _proj_call(
        x2, pw, batch=1, seq=nb, tm=nb)
    s_new, oa3 = _hgrn_step(aq.T, f.T, ai.reshape(nb, 1, w), gate.reshape(nb, 1, w), pw["gn"], state)
    scores, snew = _idx_scores(page_table, qia.reshape(nb, IDX_HEADS, IDX_DIM), iwt,
                               kia.reshape(nb, 1, 2 * IDX_DIM), cache_kidx_t, layer=layer)
    mask, newsel = _select(scores.reshape(nb, past), snew.reshape(nb, LANES), topk=topk)
    ob_t = _paged_attn(page_table, qb.astype(F32).T, kt[0], vt[0], mask.reshape(nb, n_pages, PAGE_SIZE),
                       newsel, cache_k_t, cache_v_t, layer=layer)
    y = _final_call(x2, oa3.reshape(nb, w), ob_t.T.astype(BF16), ga, gb, pw, tm=nb, alpha=alpha)
    tok = lambda a, heads: _token_major(a, 1, nb, heads)[0][:, None]
    return (y.reshape(nb, tn, d), s_new, tok(kt, B_HEADS), tok(vt, B_HEADS), tok(kit, None))


def kernel(x_prompt, x_sample, cache_k, cache_v, cache_kidx, state_hgrn, page_table, hgrn_lb_logits,
           w_in, hgrn_norm_g, w_up_a, w_up_b, w_o, ln1_g, ln1_b, w_ffn_gate, w_ffn_up, w_ffn_down,
           ln2_g, ln2_b):
    depth = w_in.shape[0]
    alpha = (2.0 * depth) ** 0.25
    lb_all = jnp.cumsum(jax.nn.softmax(hgrn_lb_logits.astype(F32), axis=0), axis=0)
    cache_k_t = cache_k.transpose(0, 1, 3, 4, 2)
    cache_v_t = cache_v.transpose(0, 1, 3, 4, 2)
    cache_kidx_t = cache_kidx.transpose(0, 1, 3, 2)
    xp, xs = x_prompt, x_sample
    outs_p, outs_s = [], []
    for l in range(depth):
        pw = _prep_weights(lb_all[l], w_in[l], hgrn_norm_g[l], w_up_a[l], w_up_b[l], w_o[l],
                           ln1_g[l], ln1_b[l], w_ffn_gate[l], w_ffn_up[l], w_ffn_down[l],
                           ln2_g[l], ln2_b[l])
        xp, sp, kp, vp, kip = _prompt_layer(xp, pw, alpha=alpha)
        xs, ss, ks, vs, kis = _sample_layer(xs, state_hgrn[l], cache_k_t, cache_v_t, cache_kidx_t,
                                            page_table, pw, alpha=alpha, layer=l)
        outs_p.append((kp, vp, kip, sp))
        outs_s.append((ks, vs, kis, ss))
    stack = lambda outs, j: jnp.stack([o[j] for o in outs], 0)
    return (xp, xs, stack(outs_p, 0), stack(outs_p, 1), stack(outs_p, 2), stack(outs_p, 3),
            stack(outs_s, 0), stack(outs_s, 1), stack(outs_s, 2), stack(outs_s, 3))
```

```python
import functools

import jax
import jax.numpy as jnp
from jax import lax
from jax.experimental import pallas as pl
from jax.experimental.pallas import tpu as pltpu

F32 = jnp.float32
BF16 = jnp.bfloat16
I32 = jnp.int32

A_HEADS = 4
A_DK = 128
A_DV = 128
B_HEADS = 8
B_HDIM = 64
IDX_HEADS = 8
IDX_DIM = 64
TOPK_MAX = 256
PAGE_SIZE = 128
LN_EPS = 1e-5
RMS_EPS = 1e-6
ATTN_SCALE = B_HDIM ** -0.5
IDX_SCALE = (IDX_DIM * IDX_HEADS) ** -0.5

LANES = 128
SUBLANES = 8
VMEM_LIMIT_BYTES = 56 * 1024 * 1024

HGRN_CHUNK = 64
HGRN_BLOCK = 16
Q_BLOCK = 128
K_BLOCK = 128
K_TILE = 256
PAGE_RING = 32

I16 = jnp.int16
HALF_BIAS = 2 ** 15
INT_MIN = -2 ** 31
NEG_INF_KEY = INT_MIN + 0x007FFFFF


def _nt(a, b):
    return lax.dot_general(a, b, (((1,), (1,)), ((), ())), preferred_element_type=F32)


def _tn(a, b):
    return lax.dot_general(a, b, (((0,), (0,)), ((), ())), preferred_element_type=F32)


def _mm(a, b):
    return jnp.dot(a, b, preferred_element_type=F32)


def _const_spec(shape):
    nd = len(shape)
    return pl.BlockSpec(shape, lambda *_: (0,) * nd, pipeline_mode=pl.Buffered(1))


def _params(*sem):
    return pltpu.CompilerParams(dimension_semantics=sem, vmem_limit_bytes=VMEM_LIMIT_BYTES)


def _float_key(x):
    bits = pltpu.bitcast(x, I32)
    return bits ^ ((bits >> 31) & 0x7FFFFFFF)


def _lane_column(x, b):
    lane = lax.broadcasted_iota(I32, x.shape, 1)
    return jnp.sum(jnp.where(lane == b, x, 0.0), axis=1, keepdims=True)


def _split3(x):
    hi = x.astype(BF16)
    r = x - hi.astype(F32)
    mid = r.astype(BF16)
    lo = (r - mid.astype(F32)).astype(BF16)
    return hi, mid, lo


def _sum6(hh, hm, mh, hl, lh, mm):
    return hh + ((hm + mh) + ((hl + lh) + mm))


def _proj_kernel(x_ref, wm_ref, wkvt_ref, wikt_ref, wacc_ref, wiwh_ref, wiwm_ref, wiwl_ref, wg_ref, lb_ref,
                 aq_ref, f_ref, ai_ref, gate_ref, qb_ref, kt_ref, kb_ref, vt_ref, vtb_ref,
                 kit_ref, qia_ref, kia_ref, iwt_ref, ga_ref, gb_ref, *, tm):
    x = x_ref[...]
    xb, xm, xl = _split3(x)
    w = A_HEADS * A_DK

    na = wacc_ref.shape[1] // 3
    t1 = _mm(xb, wacc_ref[...])
    t2 = _mm(xm, wacc_ref[:, :2 * na])
    t3 = _mm(xl, wacc_ref[:, :na])
    acc = _sum6(t1[:, :na], t1[:, na:2 * na], t2[:, :na], t1[:, 2 * na:], t3, t2[:, na:])
    qia_ref[...] = acc[:, :w]
    kia_ref[...] = acc[:, w:]
    wh, wmid, wlo = wiwh_ref[...], wiwm_ref[...], wiwl_ref[...]
    iwt_ref[...] = _sum6(_nt(wh, xb), _nt(wmid, xb), _nt(wh, xm), _nt(wlo, xb), _nt(wh, xl),
                         _nt(wmid, xm))

    def mm(j):
        return _mm(xb, wm_ref[:, j * w:(j + 1) * w])

    aq_ref[...] = mm(0)
    lb = lb_ref[...]
    f_ref[...] = lb + (1.0 - lb) * jax.nn.sigmoid(mm(1))
    ai_ref[...] = mm(2)
    ag = mm(3)
    gate_ref[...] = ag * jax.nn.sigmoid(ag)
    qb_ref[...] = (mm(4) * ATTN_SCALE).astype(BF16)
    kb_ref[...] = mm(5).astype(BF16)
    kvt = _nt(wkvt_ref[...], xb)
    kt_ref[0] = kvt[:w]
    vt = kvt[w:]
    vt_ref[0] = vt
    for r in range(tm // K_BLOCK):
        for p in range(B_HEADS // 2):
            vtb_ref[r, p] = vt[p * LANES:(p + 1) * LANES, r * K_BLOCK:(r + 1) * K_BLOCK].astype(BF16)
    kit_ref[0] = _nt(wikt_ref[...], xb)
    d = ga_ref.shape[1]
    ga_ref[...] = jax.nn.sigmoid(_mm(xb, wg_ref[:, :d]))
    gb_ref[...] = jax.nn.sigmoid(_mm(xb, wg_ref[:, d:]))


def _proj(x, wm, wkvt, wikt, wacc, wiw_h, wiw_m, wiw_l, wg, lb_row, *, batch, seq, tm):
    n, d = x.shape
    w = A_HEADS * A_DK
    nb = n // K_BLOCK
    tps = seq // tm
    row = lambda width: pl.BlockSpec((tm, width), lambda i: (i, 0))
    tmajor = lambda rows: pl.BlockSpec((1, rows, tm), lambda i: (i // tps, 0, i % tps))
    out_shape = (
        jax.ShapeDtypeStruct((n, w), F32),
        jax.ShapeDtypeStruct((n, w), F32),
        jax.ShapeDtypeStruct((n, w), F32),
        jax.ShapeDtypeStruct((n, w), F32),
        jax.ShapeDtypeStruct((n, w), BF16),
        jax.ShapeDtypeStruct((batch, w, seq), F32),
        jax.ShapeDtypeStruct((n, w), BF16),
        jax.ShapeDtypeStruct((batch, w, seq), F32),
        jax.ShapeDtypeStruct((nb, B_HEADS // 2, LANES, K_BLOCK), BF16),
        jax.ShapeDtypeStruct((batch, IDX_DIM, seq), F32),
        jax.ShapeDtypeStruct((n, w), F32),
        jax.ShapeDtypeStruct((n, 2 * IDX_DIM), F32),
        jax.ShapeDtypeStruct((IDX_HEADS, n), F32),
        jax.ShapeDtypeStruct((n, d), F32),
        jax.ShapeDtypeStruct((n, d), F32),
    )
    out_specs = (
        row(w), row(w), row(w), row(w), row(w), tmajor(w), row(w), tmajor(w),
        pl.BlockSpec((tm // K_BLOCK, B_HEADS // 2, LANES, K_BLOCK), lambda i: (i, 0, 0, 0)),
        tmajor(IDX_DIM), row(w), row(2 * IDX_DIM),
        pl.BlockSpec((IDX_HEADS, tm), lambda i: (0, i)),
        row(d), row(d),
    )
    consts = (wm, wkvt, wikt, wacc, wiw_h, wiw_m, wiw_l, wg, lb_row)
    return pl.pallas_call(
        functools.partial(_proj_kernel, tm=tm),
        grid=(n // tm,),
        in_specs=[row(d)] + [_const_spec(a.shape) for a in consts],
        out_specs=out_specs,
        out_shape=out_shape,
        compiler_params=_params("arbitrary"),
        name="proj",
    )(x, *consts)


def _hgrn_kernel(aq_ref, f_ref, ai_ref, gate_ref, gn_ref, o_ref, s_ref,
                 st_ref, *head_scratch):
    C, BS = HGRN_CHUNK, HGRN_BLOCK
    c = pl.program_id(1)
    per_head = [head_scratch[5 * h:5 * h + 5] for h in range(A_HEADS)]

    @pl.when(c == 0)
    def _():
        st_ref[...] = jnp.zeros_like(st_ref)
        for kpad, gpad, vpad, _, _ in per_head:
            kpad[...] = jnp.zeros_like(kpad)
            gpad[...] = jnp.zeros_like(gpad)
            vpad[...] = jnp.zeros_like(vpad)

    ri = lax.broadcasted_iota(I32, (C, C), 0)
    ci = lax.broadcasted_iota(I32, (C, C), 1)
    tri = jnp.where(ci <= ri, 1.0, 0.0).astype(BF16)

    row = lax.broadcasted_iota(I32, (C, A_DK), 0)
    blk = row // BS
    rr = ri // BS
    cc = ci // BS
    mask1 = (rr >= 2) & (cc < 2)
    mask2 = ((rr == 1) & (cc == 0)) | ((rr == 3) & (cc == 2))
    ones_bf = jnp.ones((A_DK, LANES), BF16)
    neg_inf = -jnp.inf

    for h in range(A_HEADS):
        sl = slice(h * A_DK, (h + 1) * A_DK)
        q = aq_ref[:, sl]
        f = f_ref[:, sl]
        v = ai_ref[:, sl]
        g = jnp.log(f)
        k = 1.0 - f
        g_hi = g.astype(BF16)
        r1 = g - g_hi.astype(F32)
        g_mid = r1.astype(BF16)
        g_lo = (r1 - g_mid.astype(F32)).astype(BF16)
        cs = _mm(tri, jnp.concatenate([g_hi, g_mid, g_lo], axis=1))
        G = cs[:, :A_DK] + cs[:, A_DK:2 * A_DK] + cs[:, 2 * A_DK:]
        g15 = G[BS - 1:BS]
        g31 = G[2 * BS - 1:2 * BS]
        g47 = G[3 * BS - 1:3 * BS]
        g_last = G[C - 1:C]

        v_bf = v.astype(BF16)
        st = st_ref[h]
        o = _nt((q * jnp.exp(G)).astype(BF16), st.astype(BF16))

        qe1 = jnp.exp(jnp.where(blk >= 2, G - g31, neg_inf))
        ke1 = jnp.exp(jnp.where(blk < 2, g31 - G, neg_inf))
        ref2 = jnp.where(blk < 2, g15, g47)
        qe2 = jnp.exp(jnp.where((blk == 1) | (blk == 3), G - ref2, neg_inf))
        ke2 = jnp.exp(jnp.where((blk == 0) | (blk == 2), ref2 - G, neg_inf))
        a1 = _nt((q * qe1).astype(BF16), (k * ke1).astype(BF16))
        a2 = _nt((q * qe2).astype(BF16), (k * ke2).astype(BF16))
        a = jnp.where(mask1, a1, 0.0) + jnp.where(mask2, a2, 0.0)
        o = o + _mm(a.astype(BF16), v_bf)

        kpad, gpad, vpad, dstack, rsum_s = per_head[h]
        kpad[BS:, :] = k
        gpad[BS:, :] = G
        vpad[BS:, :] = v
        for d in range(BS):
            ks = kpad[BS - d:BS - d + C, :]
            gs = gpad[BS - d:BS - d + C, :]
            e = jnp.exp(jnp.where((row % BS) >= d, G - gs, neg_inf))
            dstack[d * C:(d + 1) * C, :] = (q * ks * e).astype(BF16)
        rsum_s[...] = _mm(dstack[...], ones_bf)
        for d in range(BS):
            o = o + rsum_s[d * C:(d + 1) * C, :] * vpad[BS - d:BS - d + C, :]

        kt = (k * jnp.exp(g_last - G)).astype(BF16)
        st_new = st * jnp.exp(g_last) + _tn(v_bf, kt)
        st_ref[h] = st_new

        ms = jnp.mean(o * o, axis=-1, keepdims=True)
        on = o * lax.rsqrt(ms + RMS_EPS) * gn_ref[...]
        o_ref[:, sl] = (on * gate_ref[:, sl]).astype(BF16)

    @pl.when(c == pl.num_programs(1) - 1)
    def _():
        for h in range(A_HEADS):
            s_ref[0, h] = st_ref[h].T


def _hgrn_prompt(aq, f, ai, gate, gn_row, *, batch, seq):
    n, w = aq.shape
    nc = seq // HGRN_CHUNK
    blk = pl.BlockSpec((HGRN_CHUNK, w), lambda b, c: (b * nc + c, 0))
    pad_rows = HGRN_BLOCK + HGRN_CHUNK
    return pl.pallas_call(
        _hgrn_kernel,
        grid=(batch, nc),
        in_specs=[blk, blk, blk, blk, _const_spec(gn_row.shape)],
        out_specs=(blk, pl.BlockSpec((1, A_HEADS, A_DK, A_DV), lambda b, c: (b, 0, 0, 0))),
        out_shape=(jax.ShapeDtypeStruct((n, w), BF16),
                   jax.ShapeDtypeStruct((batch, A_HEADS, A_DK, A_DV), F32)),
        scratch_shapes=[pltpu.VMEM((A_HEADS, A_DV, A_DK), F32)] + A_HEADS * [
            pltpu.VMEM((pad_rows, A_DK), F32),
            pltpu.VMEM((pad_rows, A_DK), F32),
            pltpu.VMEM((pad_rows, A_DV), F32),
            pltpu.VMEM((HGRN_BLOCK * HGRN_CHUNK, A_DK), BF16),
            pltpu.VMEM((HGRN_BLOCK * HGRN_CHUNK, LANES), F32)],
        compiler_params=_params("arbitrary", "arbitrary"),
        name="hgrn_prompt",
    )(aq, f, ai, gate, gn_row)


def _count_rows(key_s, nkt, hits, n_out=1):
    def body(kt, accs):
        off = pl.multiple_of(kt * K_TILE, K_TILE)
        hs = hits(key_s[pl.ds(off, K_TILE), :], kt)
        return tuple(a + h.reshape(K_TILE // SUBLANES, SUBLANES, Q_BLOCK).sum(axis=0)
                     for a, h in zip(accs, hs))
    accs = lax.fori_loop(0, nkt, body, tuple(jnp.zeros((SUBLANES, Q_BLOCK), I32) for _ in range(n_out)))
    return tuple(a.sum(axis=0, keepdims=True) for a in accs)


def _count_ge16(half_s, nkt, cand):
    rows = 2 * SUBLANES
    one, zero = jnp.int16(1), jnp.int16(0)

    def body(kt, acc):
        off = pl.multiple_of(kt * K_TILE, K_TILE)
        hit = jnp.where(half_s[pl.ds(off, K_TILE), :] >= cand, one, zero)
        hit = hit.reshape(K_TILE // rows, rows, Q_BLOCK)
        parts = [hit[j] for j in range(K_TILE // rows)]
        while len(parts) > 1:
            parts = [a + b for a, b in zip(parts[::2], parts[1::2])]
        return acc + parts[0]
    acc = lax.fori_loop(0, nkt, body, jnp.zeros((rows, Q_BLOCK), I16))
    return acc.astype(I32).sum(axis=0, keepdims=True)


def _swap_halves(x):
    return jnp.concatenate([x[:, B_HDIM:], x[:, :B_HDIM]], axis=1)


def _attn_kernel(q_ref, qia_ref, iwt_ref, k_ref, kia_ref, vt_ref, o_ref,
                 kaug, kst1, kst2, r1_s, r2_s, rs_s, key_s, hi_s, lo_s, bias_s, lg_s, ot_s, *, seq, topk):
    i = pl.program_id(1)
    nkt = (i * Q_BLOCK) // K_TILE + 1
    npair = B_HEADS // 2
    lane = lax.broadcasted_iota(I32, (Q_BLOCK, LANES), 1)
    krow = lax.broadcasted_iota(I32, (K_TILE, Q_BLOCK), 0)
    qpos = i * Q_BLOCK + lax.broadcasted_iota(I32, (K_TILE, Q_BLOCK), 1)

    @pl.when(i == 0)
    def _():
        def body(kb, carry):
            off = pl.multiple_of(kb * K_BLOCK, K_BLOCK)
            pos = off + lax.broadcasted_iota(I32, (K_BLOCK, LANES), 0)
            ln = lax.broadcasted_iota(I32, (K_BLOCK, LANES), 1)
            feat = jnp.where(ln == 0, (pos // 64) * 64, jnp.where(ln == 1, pos % 64, 0))
            feat = feat.astype(F32).astype(BF16)
            for p in range(npair):
                kaug[p, pl.ds(off, K_BLOCK), :LANES] = k_ref[pl.ds(off, K_BLOCK), p * LANES:(p + 1) * LANES]
                kaug[p, pl.ds(off, K_BLOCK), LANES:] = feat
            kh, km, kl = _split3(kia_ref[pl.ds(off, K_BLOCK), :])
            half = ln < IDX_DIM
            kst1[pl.ds(off, K_BLOCK), :LANES] = kh
            kst1[pl.ds(off, K_BLOCK), LANES:] = jnp.where(half, km, kh)
            kst2[pl.ds(off, K_BLOCK), :] = jnp.where(half, kl, km)
            return carry
        lax.fori_loop(0, seq // K_BLOCK, body, 0)

    zero_bf = jnp.zeros((Q_BLOCK, LANES), BF16)
    for p in range(npair):
        qp = q_ref[:, p * LANES:(p + 1) * LANES]
        lo = lane < B_HDIM
        qh, qm, ql = _split3(qia_ref[:, p * LANES:(p + 1) * LANES])
        rqh = _swap_halves(qh)
        a0 = jnp.where(lo, qh, _swap_halves(qm))
        b0 = jnp.where(lo, rqh, qm)
        r1_s[p, :Q_BLOCK, :LANES] = a0
        r1_s[p, :Q_BLOCK, LANES:] = jnp.where(lo, qh, _swap_halves(ql))
        r1_s[p, Q_BLOCK:, :LANES] = b0
        r1_s[p, Q_BLOCK:, LANES:] = jnp.where(lo, rqh, ql)
        r2_s[p, :Q_BLOCK, :] = a0
        r2_s[p, Q_BLOCK:, :] = b0
        rs_s[p, :Q_BLOCK, :LANES] = jnp.where(lo, qp, zero_bf)
        rs_s[p, Q_BLOCK:, :LANES] = jnp.where(lo, zero_bf, qp)
        m0 = 2.0 ** -(2 * p + 1)
        m1 = 2.0 ** -(2 * p + 2)
        rs_s[p, :Q_BLOCK, LANES:] = jnp.where(lane < 2, m0, 0.0).astype(BF16)
        rs_s[p, Q_BLOCK:, LANES:] = jnp.where(lane < 2, m1, 0.0).astype(BF16)

    def score_body(kt, carry):
        off = pl.multiple_of(kt * K_TILE, K_TILE)
        k1 = kst1[pl.ds(off, K_TILE), :]
        k2 = kst2[pl.ds(off, K_TILE), :]
        acc = jnp.zeros((K_TILE, Q_BLOCK), F32)
        for p in range(npair):
            s2 = _nt(k1, r1_s[p]) + _nt(k2, r2_s[p])
            for hh in range(2):
                hd = 2 * p + hh
                acc = acc + iwt_ref[hd:hd + 1, :] * jnp.maximum(s2[:, hh * Q_BLOCK:(hh + 1) * Q_BLOCK], 0.0)
        sc = jnp.where(off + krow <= qpos, acc * IDX_SCALE, -jnp.inf)
        key = _float_key(sc)
        key_s[pl.ds(off, K_TILE), :] = key
        hi_s[pl.ds(off, K_TILE), :] = (key >> 16).astype(I16)
        return carry
    lax.fori_loop(0, nkt, score_body, 0)

    def search16(half_s, base):
        def bit_body(it, ans):
            cand = ans | lax.shift_left(jnp.int32(1), 15 - it)
            cnt = base + _count_ge16(half_s, nkt, (cand - HALF_BIAS).astype(I16))
            return jnp.where(cnt >= topk, cand, ans)
        return lax.fori_loop(0, 16, bit_body, jnp.zeros((1, Q_BLOCK), I32))
    hi = search16(hi_s, 0) - HALF_BIAS
    above = jnp.where(hi < HALF_BIAS - 1,
                      _count_ge16(hi_s, nkt, jnp.minimum(hi + 1, HALF_BIAS - 1).astype(I16)), 0)

    def lo_body(kt, carry):
        off = pl.multiple_of(kt * K_TILE, K_TILE)
        blk = key_s[pl.ds(off, K_TILE), :]
        lo = jnp.where((blk >> 16) == hi, (blk & 0xFFFF) - HALF_BIAS, -HALF_BIAS)
        lo_s[pl.ds(off, K_TILE), :] = lo.astype(I16)
        return carry
    lax.fori_loop(0, nkt, lo_body, 0)
    thr = hi * (2 * HALF_BIAS) + search16(lo_s, above)
    thr_gt = jnp.maximum(thr, NEG_INF_KEY)
    n_gt, n_eq = _count_rows(
        key_s, nkt, lambda blk, kt: (jnp.where(blk > thr_gt, 1, 0), jnp.where(blk == thr, 1, 0)), n_out=2)
    need = topk - n_gt

    def tie_search():
        def tie_body(it, y):
            cand = y | lax.shift_left(jnp.int32(1), 11 - it)
            cnt, = _count_rows(
                key_s, nkt,
                lambda blk, kt: (jnp.where(blk == thr, jnp.where(kt * K_TILE + krow < cand, 1, 0), 0),))
            return jnp.where(cnt < need, cand, y)
        return lax.fori_loop(0, 12, tie_body, jnp.zeros((1, Q_BLOCK), I32))
    surplus = jnp.max(jnp.where(thr > NEG_INF_KEY, n_eq - need, 0))
    y = lax.cond(surplus > 0, tie_search, lambda: jnp.full((1, Q_BLOCK), 4095, I32))
    ylim = jnp.where(thr <= NEG_INF_KEY, 0, y + 1)

    def bias_body(kt, carry):
        off = pl.multiple_of(kt * K_TILE, K_TILE)
        blk = key_s[pl.ds(off, K_TILE), :]
        tie = jnp.where(blk == thr, jnp.where(off + krow < ylim, 0.0, -jnp.inf), -jnp.inf)
        bias_s[pl.ds(off, K_TILE), :] = jnp.where(blk > thr_gt, 0.0, tie)
        return carry
    lax.fori_loop(0, nkt, bias_body, 0)

    def fold(x, op):
        return op(x.reshape(x.shape[0] // SUBLANES, SUBLANES, 2 * Q_BLOCK), axis=0)

    def pass1(kt, mxs):
        off = pl.multiple_of(kt * K_TILE, K_TILE)
        b = bias_s[pl.ds(off, K_TILE), :]
        b2 = jnp.concatenate([b, b], axis=1)
        out = []
        for p in range(npair):
            lg = _nt(kaug[p, pl.ds(off, K_TILE), :], rs_s[p]) + b2
            lg_s[p, pl.ds(off, K_TILE), :] = lg
            out.append(jnp.maximum(mxs[p], fold(lg, jnp.max)))
        return tuple(out)
    mxs = lax.fori_loop(0, nkt, pass1, tuple(jnp.full((SUBLANES, 2 * Q_BLOCK), -jnp.inf, F32)
                                             for _ in range(npair)))
    ms = [mx.max(axis=0, keepdims=True) for mx in mxs]

    ot_s[...] = jnp.zeros_like(ot_s)

    def pass2(kt, l8s):
        off = pl.multiple_of(kt * K_TILE, K_TILE)
        nvb = K_TILE // K_BLOCK
        out = []
        for p in range(npair):
            pe = jnp.exp(lg_s[p, pl.ds(off, K_TILE), :] - ms[p])
            vt = jnp.concatenate([vt_ref[kt * nvb + u, p] for u in range(nvb)], axis=1)
            ot_s[p] += _mm(vt, pe.astype(BF16))
            out.append(l8s[p] + fold(pe, jnp.sum))
        return tuple(out)
    l8s = lax.fori_loop(0, nkt, pass2, tuple(jnp.zeros((SUBLANES, 2 * Q_BLOCK), F32) for _ in range(npair)))
    for p in range(npair):
        inv = 1.0 / l8s[p].sum(axis=0, keepdims=True)
        ot = ot_s[p]
        top = (ot[:B_HDIM, :Q_BLOCK] * inv[:, :Q_BLOCK]).T
        bot = (ot[B_HDIM:, Q_BLOCK:] * inv[:, Q_BLOCK:]).T
        o_ref[:, p * LANES:(p + 1) * LANES] = jnp.concatenate([top, bot], axis=1).astype(BF16)


def _attn_prompt(qb, qia, iwt, kb, kia, vtb, *, batch, seq, topk):
    n, w = qb.shape
    nq = seq // Q_BLOCK
    npair = B_HEADS // 2
    assert seq % K_TILE == 0 and K_TILE % K_BLOCK == 0
    qspec = pl.BlockSpec((Q_BLOCK, w), lambda b, i: (b * nq + i, 0))
    return pl.pallas_call(
        functools.partial(_attn_kernel, seq=seq, topk=topk),
        grid=(batch, nq),
        in_specs=[qspec, qspec,
                  pl.BlockSpec((IDX_HEADS, Q_BLOCK), lambda b, i: (0, b * nq + i)),
                  pl.BlockSpec((seq, w), lambda b, i: (b, 0)),
                  pl.BlockSpec((seq, 2 * IDX_DIM), lambda b, i: (b, 0)),
                  pl.BlockSpec((seq // K_BLOCK, npair, LANES, K_BLOCK), lambda b, i: (b, 0, 0, 0))],
        out_specs=qspec,
        out_shape=jax.ShapeDtypeStruct((n, w), BF16),
        scratch_shapes=[pltpu.VMEM((npair, seq, 2 * LANES), BF16),
                        pltpu.VMEM((seq, 2 * LANES), BF16),
                        pltpu.VMEM((seq, LANES), BF16),
                        pltpu.VMEM((npair, 2 * Q_BLOCK, 2 * LANES), BF16),
                        pltpu.VMEM((npair, 2 * Q_BLOCK, LANES), BF16),
                        pltpu.VMEM((npair, 2 * Q_BLOCK, 2 * LANES), BF16),
                        pltpu.VMEM((seq, Q_BLOCK), I32),
                        pltpu.VMEM((seq, Q_BLOCK), I16),
                        pltpu.VMEM((seq, Q_BLOCK), I16),
                        pltpu.VMEM((seq, Q_BLOCK), F32),
                        pltpu.VMEM((npair, seq, 2 * Q_BLOCK), F32),
                        pltpu.VMEM((npair, LANES, 2 * Q_BLOCK), F32)],
        compiler_params=_params("arbitrary", "arbitrary"),
        name="attn_prompt",
    )(qb, qia, iwt, kb, kia, vtb)


def _layer_norm(x, g, b):
    mu = jnp.mean(x, axis=-1, keepdims=True)
    xc = x - mu
    var = jnp.mean(xc * xc, axis=-1, keepdims=True)
    return xc * lax.rsqrt(var + LN_EPS) * g + b


def _final_kernel(x_ref, oa_ref, ob_ref, ga_ref, gb_ref, wua_ref, wub_ref, wo_ref,
                  l1g_ref, l1b_ref, wfg_ref, wfu_ref, wfd_ref, l2g_ref, l2b_ref, y_ref,
                  *, alpha, ff_chunk):
    ya = _mm(oa_ref[...], wua_ref[...])
    yb = _mm(ob_ref[...], wub_ref[...])
    merged = (ga_ref[...] * ya + gb_ref[...] * yb).astype(BF16)
    mix = _mm(merged, wo_ref[...])
    x1 = _layer_norm(alpha * x_ref[...] + mix, l1g_ref[...], l1b_ref[...])
    x1b = x1.astype(BF16)
    d_ff = wfg_ref.shape[1]
    acc = jnp.zeros(x1.shape, F32)
    for c in range(d_ff // ff_chunk):
        cs = slice(c * ff_chunk, (c + 1) * ff_chunk)
        hg = _mm(x1b, wfg_ref[:, cs])
        hu = _mm(x1b, wfu_ref[:, cs])
        hc = (hg * jax.nn.sigmoid(hg) * hu).astype(BF16)
        acc = acc + _mm(hc, wfd_ref[cs, :])
    y_ref[...] = _layer_norm(alpha * x1 + acc, l2g_ref[...], l2b_ref[...])


def _final(x, oa, ob, ga, gb, wua, wub, wo, l1g, l1b, wfg, wfu, wfd, l2g, l2b, *, tm, alpha):
    n, d = x.shape
    w = oa.shape[1]
    row = lambda width: pl.BlockSpec((tm, width), lambda i: (i, 0))
    consts = (wua, wub, wo, l1g, l1b, wfg, wfu, wfd, l2g, l2b)
    return pl.pallas_call(
        functools.partial(_final_kernel, alpha=alpha, ff_chunk=256),
        grid=(n // tm,),
        in_specs=[row(d), row(w), row(w), row(d), row(d)] + [_const_spec(a.shape) for a in consts],
        out_specs=row(d),
        out_shape=jax.ShapeDtypeStruct((n, d), F32),
        compiler_params=_params("arbitrary"),
        name="merge_ffn",
    )(x, oa, ob, ga, gb, *consts)


def _hgrn_step_kernel(qt_ref, ft_ref, v_ref, gate_ref, gn_ref, s_ref, so_ref, o_ref, *, per_step):
    for j in range(per_step):
        b = pl.program_id(0) * per_step + j
        qcol = _lane_column(qt_ref[...], b)
        fcol = _lane_column(ft_ref[...], b)
        for h in range(A_HEADS):
            sl = slice(h * A_DK, (h + 1) * A_DK)
            fc = fcol[sl]
            vrow = v_ref[j, :, sl]
            s_new = fc * s_ref[j, h] + (1.0 - fc) * vrow
            so_ref[j, h] = s_new
            o = jnp.sum(qcol[sl] * s_new, axis=0, keepdims=True)
            ms = jnp.mean(o * o, axis=-1, keepdims=True)
            on = o * lax.rsqrt(ms + RMS_EPS) * gn_ref[...]
            o_ref[j, :, sl] = (on * gate_ref[j, :, sl]).astype(BF16)


def _hgrn_step(aq_t, f_t, ai3, gate3, gn_row, state):
    nb = state.shape[0]
    w = A_HEADS * A_DK
    per_step = 4 if nb % 4 == 0 else 1
    sspec = pl.BlockSpec((per_step, A_HEADS, A_DK, A_DV), lambda b: (b, 0, 0, 0))
    rspec = pl.BlockSpec((per_step, 1, w), lambda b: (b, 0, 0))
    return pl.pallas_call(
        functools.partial(_hgrn_step_kernel, per_step=per_step),
        grid=(nb // per_step,),
        in_specs=[_const_spec(aq_t.shape), _const_spec(f_t.shape), rspec, rspec,
                  _const_spec(gn_row.shape), sspec],
        out_specs=(sspec, rspec),
        out_shape=(jax.ShapeDtypeStruct(state.shape, F32),
                   jax.ShapeDtypeStruct((nb, 1, w), BF16)),
        compiler_params=_params("arbitrary"),
        name="hgrn_step",
    )(aq_t, f_t, ai3, gate3, gn_row, state)


def _page_copies(pt_ref, cache_ref, buf, sem, b, slot, n_pages, layer):
    def body(pg, carry):
        pltpu.make_async_copy(cache_ref.at[layer, pt_ref[b, pg]], buf.at[slot, pg], sem.at[slot]).start()
        return carry
    lax.fori_loop(0, n_pages, body, 0)


def _idx_score_kernel(pt_ref, qi_ref, iwt_ref, kn_ref, cache_ref, sc_ref, sn_ref, buf, sem,
                      *, n_pages, layer):
    b = pl.program_id(0)
    nb = pl.num_programs(0)
    slot = b % 2

    @pl.when(b == 0)
    def _():
        _page_copies(pt_ref, cache_ref, buf, sem, b, slot, n_pages, layer)

    @pl.when(b + 1 < nb)
    def _():
        _page_copies(pt_ref, cache_ref, buf, sem, b + 1, 1 - slot, n_pages, layer)

    qi = qi_ref[0]
    wcol = _lane_column(iwt_ref[...], b)

    def head_sum(s):
        return jnp.sum(wcol * jnp.maximum(s, 0.0), axis=0, keepdims=True) * IDX_SCALE

    sn = jnp.sum(qi * kn_ref[0][:, :IDX_DIM], axis=1, keepdims=True)
    sn_ref[0] = jnp.broadcast_to(head_sum(sn), (1, LANES))

    qh, qm, ql = _split3(qi)
    lhs1 = jnp.concatenate([qh, qm, qh, ql], axis=1)
    lhs2 = jnp.concatenate([qh, qm], axis=1)

    def page_scores(tile):
        kh, km, kl = _split3(tile)
        s = _mm(lhs1, jnp.concatenate([kh, kh, km, kh], axis=0))
        return head_sum(s + _mm(lhs2, jnp.concatenate([kl, km], axis=0)))

    pltpu.make_async_copy(buf.at[slot], buf.at[slot], sem.at[slot]).wait()
    for g in range(n_pages // SUBLANES):
        rows = [page_scores(buf[slot, g * SUBLANES + r]) for r in range(SUBLANES)]
        sc_ref[0, g * SUBLANES:(g + 1) * SUBLANES, :] = jnp.concatenate(rows, axis=0)


def _idx_scores(page_table, qi3, iwt, kidx_new3, cache_kidx_t, *, layer):
    nb, n_pages = page_table.shape
    assert n_pages % SUBLANES == 0
    grid_spec = pltpu.PrefetchScalarGridSpec(
        num_scalar_prefetch=1,
        grid=(nb,),
        in_specs=[pl.BlockSpec((1, IDX_HEADS, IDX_DIM), lambda b, pt: (b, 0, 0)),
                  _const_spec(iwt.shape),
                  pl.BlockSpec((1, 1, 2 * IDX_DIM), lambda b, pt: (b, 0, 0)),
                  pl.BlockSpec(memory_space=pl.ANY)],
        out_specs=(pl.BlockSpec((1, n_pages, PAGE_SIZE), lambda b, pt: (b, 0, 0)),
                   pl.BlockSpec((1, 1, LANES), lambda b, pt: (b, 0, 0))),
        scratch_shapes=[pltpu.VMEM((2, n_pages, IDX_DIM, PAGE_SIZE), F32),
                        pltpu.SemaphoreType.DMA((2,))],
    )
    return pl.pallas_call(
        functools.partial(_idx_score_kernel, n_pages=n_pages, layer=layer),
        grid_spec=grid_spec,
        out_shape=(jax.ShapeDtypeStruct((nb, n_pages, PAGE_SIZE), F32),
                   jax.ShapeDtypeStruct((nb, 1, LANES), F32)),
        compiler_params=_params("arbitrary"),
        name="idx_scores",
    )(page_table, qi3, iwt, kidx_new3, cache_kidx_t)


def _select_kernel(sc_ref, sn_ref, mask_ref, newsel_ref, key_s, *, topk, pos_bits):
    nb, past = sc_ref.shape
    ncol = past // LANES
    key_s[...] = _float_key(sc_ref[...])
    kn = _float_key(sn_ref[...])
    pos = lax.broadcasted_iota(I32, (nb, LANES), 1)

    def count(hits_past, hit_new):
        def body(j, acc):
            off = pl.multiple_of(j * LANES, LANES)
            return acc + hits_past(key_s[:, pl.ds(off, LANES)], off)
        acc = lax.fori_loop(0, ncol, body, jnp.zeros((nb, LANES), I32))
        tot = jnp.sum(acc, axis=1, keepdims=True)
        return jnp.broadcast_to(tot, (nb, LANES)) + hit_new

    def bit_body(it, ans):
        cand = ans | lax.shift_left(jnp.int32(1), 31 - it)
        c = cand ^ INT_MIN
        cnt = count(lambda blk, off: jnp.where(blk >= c, 1, 0), jnp.where(kn >= c, 1, 0))
        return jnp.where(cnt >= topk, cand, ans)
    ans = lax.fori_loop(0, 32, bit_body, jnp.zeros((nb, LANES), I32))
    thr = ans ^ INT_MIN
    thr_gt = jnp.maximum(thr, NEG_INF_KEY)
    need = topk - count(lambda blk, off: jnp.where(blk > thr_gt, 1, 0), jnp.where(kn > thr_gt, 1, 0))
    n_eq = count(lambda blk, off: jnp.where(blk == thr, 1, 0), jnp.where(kn == thr, 1, 0))

    def tie_search():
        def tie_body(it, y):
            cand = y | lax.shift_left(jnp.int32(1), pos_bits - 1 - it)
            cnt = count(lambda blk, off: jnp.where(blk == thr, jnp.where(off + pos < cand, 1, 0), 0),
                        jnp.where(kn == thr, jnp.where(past < cand, 1, 0), 0))
            return jnp.where(cnt < need, cand, y)
        return lax.fori_loop(0, pos_bits, tie_body, jnp.zeros((nb, LANES), I32))
    surplus = jnp.max(jnp.where(thr > NEG_INF_KEY, n_eq - need, 0))
    y = lax.cond(surplus > 0, tie_search, lambda: jnp.full((nb, LANES), 2 ** pos_bits - 1, I32))
    ylim = jnp.where(thr <= NEG_INF_KEY, 0, y + 1)

    def out_body(j, carry):
        off = pl.multiple_of(j * LANES, LANES)
        blk = key_s[:, pl.ds(off, LANES)]
        tie = jnp.where(blk == thr, jnp.where(off + pos < ylim, 1.0, 0.0), 0.0)
        mask_ref[:, pl.ds(off, LANES)] = jnp.where(blk > thr_gt, 1.0, tie)
        return carry
    lax.fori_loop(0, ncol, out_body, 0)
    tie_new = jnp.where(kn == thr, jnp.where(past < ylim, 1, 0), 0)
    newsel_ref[...] = jnp.where(kn > thr_gt, 1, tie_new)


def _select(scores, snew, *, topk):
    nb, past = scores.shape
    pos_bits = (past + 1).bit_length()
    return pl.pallas_call(
        functools.partial(_select_kernel, topk=topk, pos_bits=pos_bits),
        out_shape=(jax.ShapeDtypeStruct((nb, past), F32),
                   jax.ShapeDtypeStruct((nb, LANES), I32)),
        scratch_shapes=[pltpu.VMEM((nb, past), I32)],
        compiler_params=pltpu.CompilerParams(vmem_limit_bytes=VMEM_LIMIT_BYTES),
        name="topk_select",
    )(scores, snew)


def _paged_attn_kernel(pt_ref, qt_ref, knt_ref, vnt_ref, mask_ref, nsel_ref, ck_ref, cv_ref, ot_ref,
                       kring, vring, sem, qcol_s, p_s, acc_s, *, n_pages, ring, layer):
    b = pl.program_id(0)
    nb = pl.num_programs(0)
    past = n_pages * PAGE_SIZE

    def page_copy(cache_ref, buf, which, bb, pg):
        return pltpu.make_async_copy(cache_ref.at[layer, pt_ref[bb, pg]], buf.at[pg % ring],
                                     sem.at[which, pg % ring])

    def start_ahead(cache_ref, buf, which, pg):
        n = pg + ring
        wrap = n >= n_pages
        bb = jnp.where(wrap, b + 1, b)
        pg2 = jnp.where(wrap, n - n_pages, n)

        @pl.when(bb < nb)
        def _():
            page_copy(cache_ref, buf, which, bb, pg2).start()

    @pl.when(b == 0)
    def _():
        ot_ref[...] = jnp.zeros_like(ot_ref)
        for r in range(ring):
            page_copy(ck_ref, kring, 0, 0, r).start()
            page_copy(cv_ref, vring, 1, 0, r).start()

    qcol = _lane_column(qt_ref[...], b)
    kncol = _lane_column(knt_ref[...], b)
    vncol = _lane_column(vnt_ref[...], b)
    for h in range(B_HEADS):
        qcol_s[h] = jnp.broadcast_to(qcol[h * B_HDIM:(h + 1) * B_HDIM], (B_HDIM, PAGE_SIZE))

    sub = lax.broadcasted_iota(I32, (B_HEADS, PAGE_SIZE), 0)
    lane = lax.broadcasted_iota(I32, (B_HEADS, PAGE_SIZE), 1)
    slopes = jnp.exp2(-(sub + 1).astype(F32))

    def head_rows(tile_of_head):
        out = jnp.zeros((B_HEADS, PAGE_SIZE), F32)
        for h in range(B_HEADS):
            out = jnp.where(sub == h, jnp.sum(tile_of_head(h), axis=0, keepdims=True), out)
        return out

    def k_body(pg, mx):
        page_copy(ck_ref, kring, 0, b, pg).wait()
        lg = head_rows(lambda h: kring[pg % ring, h] * qcol_s[h])
        dist = (past - pg * PAGE_SIZE - lane).astype(F32)
        lg = jnp.where(mask_ref[0, pl.ds(pg, 1), :] > 0.0, lg - slopes * dist, -jnp.inf)
        p_s[pg] = lg
        start_ahead(ck_ref, kring, 0, pg)
        return jnp.maximum(mx, lg)
    mx = lax.fori_loop(0, n_pages, k_body, jnp.full((B_HEADS, PAGE_SIZE), -jnp.inf, F32))

    lgn = head_rows(lambda h: qcol_s[h] * kncol[h * B_HDIM:(h + 1) * B_HDIM])
    lgn = jnp.where(nsel_ref[pl.ds(b, 1), :] > 0, lgn, -jnp.inf)
    m = jnp.maximum(jnp.max(mx, axis=1, keepdims=True), lgn)
    pn = jnp.exp(lgn - m)

    def e_body(pg, l):
        pe = jnp.exp(p_s[pg] - m)
        p_s[pg] = pe
        return l + pe
    lsum = lax.fori_loop(0, n_pages, e_body, jnp.zeros((B_HEADS, PAGE_SIZE), F32))
    inv = 1.0 / (jnp.sum(lsum, axis=1, keepdims=True) + pn)

    acc_s[...] = jnp.zeros_like(acc_s)

    def v_body(pg, carry):
        page_copy(cv_ref, vring, 1, b, pg).wait()
        pe = p_s[pg]
        for h in range(B_HEADS):
            acc_s[h] += vring[pg % ring, h] * pe[h:h + 1, :]
        start_ahead(cv_ref, vring, 1, pg)
        return carry
    lax.fori_loop(0, n_pages, v_body, 0)

    lane_o = lax.broadcasted_iota(I32, (B_HDIM, ot_ref.shape[1]), 1)
    for h in range(B_HEADS):
        hs = slice(h * B_HDIM, (h + 1) * B_HDIM)
        oc = jnp.sum(acc_s[h], axis=1, keepdims=True) + pn[h:h + 1, :1] * vncol[hs]
        oc = oc * inv[h:h + 1, :1]
        ot_ref[hs, :] = jnp.where(lane_o == b, oc, ot_ref[hs, :])


def _paged_attn(page_table, q_t, kn_t, vn_t, mask3, newsel, cache_k_t, cache_v_t, *, layer):
    nb, n_pages = page_table.shape
    ring = min(PAGE_RING, n_pages)
    assert n_pages % ring == 0
    w = B_HEADS * B_HDIM
    page = (B_HEADS, B_HDIM, PAGE_SIZE)
    grid_spec = pltpu.PrefetchScalarGridSpec(
        num_scalar_prefetch=1,
        grid=(nb,),
        in_specs=[_const_spec(q_t.shape), _const_spec(kn_t.shape), _const_spec(vn_t.shape),
                  pl.BlockSpec((1, n_pages, PAGE_SIZE), lambda b, pt: (b, 0, 0)),
                  _const_spec(newsel.shape),
                  pl.BlockSpec(memory_space=pl.ANY),
                  pl.BlockSpec(memory_space=pl.ANY)],
        out_specs=pl.BlockSpec((w, nb), lambda b, pt: (0, 0)),
        scratch_shapes=[pltpu.VMEM((ring,) + page, F32),
                        pltpu.VMEM((ring,) + page, F32),
                        pltpu.SemaphoreType.DMA((2, ring)),
                        pltpu.VMEM(page, F32),
                        pltpu.VMEM((n_pages, B_HEADS, PAGE_SIZE), F32),
                        pltpu.VMEM(page, F32)],
    )
    return pl.pallas_call(
        functools.partial(_paged_attn_kernel, n_pages=n_pages, ring=ring, layer=layer),
        grid_spec=grid_spec,
        out_shape=jax.ShapeDtypeStruct((w, nb), F32),
        compiler_params=_params("arbitrary"),
        name="paged_attn",
    )(page_table, q_t, kn_t, vn_t, mask3, newsel, cache_k_t, cache_v_t)


def _prep_weights(lb, w_in, hgrn_norm_g, w_up_a, w_up_b, w_o, ln1_g, ln1_b,
                  w_ffn_gate, w_ffn_up, w_ffn_down, ln2_g, ln2_b):
    w = A_HEADS * A_DK
    o_k, o_v, o_iq = 5 * w, 6 * w, 7 * w
    o_ik = 8 * w
    o_iw = o_ik + IDX_DIM
    o_g = o_iw + IDX_HEADS
    wb = w_in.astype(BF16)
    wt = w_in.T.astype(BF16)
    row = lambda a: a.reshape(1, -1).astype(F32)

    def split3(a):
        hi = a.astype(BF16)
        r = a - hi.astype(F32)
        mid = r.astype(BF16)
        return hi, mid, (r - mid.astype(F32)).astype(BF16)

    w_idx = jnp.concatenate([w_in[:, o_iq:o_ik], w_in[:, o_ik:o_iw], w_in[:, o_ik:o_iw]], axis=1)
    wiw_h, wiw_m, wiw_l = split3(w_in[:, o_iw:o_g].T)
    return dict(
        wm=wb[:, :o_v],
        wkvt=wt[o_k:o_v + w],
        wikt=wt[o_ik:o_iw],
        wacc=jnp.concatenate(split3(w_idx), axis=1),
        wiw_h=wiw_h, wiw_m=wiw_m, wiw_l=wiw_l,
        wg=wb[:, o_g:],
        lb=row(lb),
        gn=row(hgrn_norm_g),
        wua=w_up_a.astype(BF16), wub=w_up_b.astype(BF16), wo=w_o.astype(BF16),
        l1g=row(ln1_g), l1b=row(ln1_b),
        wfg=w_ffn_gate.astype(BF16), wfu=w_ffn_up.astype(BF16), wfd=w_ffn_down.astype(BF16),
        l2g=row(ln2_g), l2b=row(ln2_b),
    )


def _proj_call(x2, pw, *, batch, seq, tm):
    return _proj(x2, pw["wm"], pw["wkvt"], pw["wikt"], pw["wacc"], pw["wiw_h"], pw["wiw_m"], pw["wiw_l"],
                 pw["wg"], pw["lb"], batch=batch, seq=seq, tm=tm)


def _final_call(x2, oa, ob, ga, gb, pw, *, tm, alpha):
    return _final(x2, oa, ob, ga, gb, pw["wua"], pw["wub"], pw["wo"], pw["l1g"], pw["l1b"],
                  pw["wfg"], pw["wfu"], pw["wfd"], pw["l2g"], pw["l2b"], tm=tm, alpha=alpha)


def _token_major(feat_t, batch, seq, heads):
    if heads is None:
        return feat_t.transpose(0, 2, 1)
    return feat_t.reshape(batch, heads, -1, seq).transpose(0, 3, 1, 2)


def _prompt_layer(x, pw, *, alpha):
    batch, seq, d = x.shape
    n = batch * seq
    x2 = x.reshape(n, d)
    (aq, f, ai, gate, qb, kt, kb, vt, vtb, kit, qia, kia, iwt, ga, gb) = _proj_call(
        x2, pw, batch=batch, seq=seq, tm=256)
    oa, s_new = _hgrn_prompt(aq, f, ai, gate, pw["gn"], batch=batch, seq=seq)
    ob = _attn_prompt(qb, qia, iwt, kb, kia, vtb, batch=batch, seq=seq, topk=min(TOPK_MAX, seq // 4))
    y = _final_call(x2, oa, ob, ga, gb, pw, tm=512, alpha=alpha)
    return (y.reshape(batch, seq, d), s_new,
            _token_major(kt, batch, seq, B_HEADS), _token_major(vt, batch, seq, B_HEADS),
            _token_major(kit, batch, seq, None))


def _sample_layer(x, state, cache_k_t, cache_v_t, cache_kidx_t, page_table, pw, *, alpha, layer):
    nb, tn, d = x.shape
    n_pages = page_table.shape[1]
    past = n_pages * PAGE_SIZE
    topk = min(TOPK_MAX, (past + tn) // 4)
    w = A_HEADS * A_DK
    x2 = x.reshape(nb, d)
    (aq, f, ai, gate, qb, kt, kb, vt, vtb, kit, qia, kia, iwt, ga, gb) = _proj_call(
        x2, pw, batch=1, seq=nb, tm=nb)
    s_new, oa3 = _hgrn_step(aq.T, f.T, ai.reshape(nb, 1, w), gate.reshape(nb, 1, w), pw["gn"], state)
    scores, snew = _idx_scores(page_table, qia.reshape(nb, IDX_HEADS, IDX_DIM), iwt,
                               kia.reshape(nb, 1, 2 * IDX_DIM), cache_kidx_t, layer=layer)
    mask, newsel = _select(scores.reshape(nb, past), snew.reshape(nb, LANES), topk=topk)
    ob_t = _paged_attn(page_table, qb.astype(F32).T, kt[0], vt[0], mask.reshape(nb, n_pages, PAGE_SIZE),
                       newsel, cache_k_t, cache_v_t, layer=layer)
    y = _final_call(x2, oa3.reshape(nb, w), ob_t.T.astype(BF16), ga, gb, pw, tm=nb, alpha=alpha)
    tok = lambda a, heads: _token_major(a, 1, nb, heads)[0][:, None]
    return (y.reshape(nb, tn, d), s_new, tok(kt, B_HEADS), tok(vt, B_HEADS), tok(kit, None))


def kernel(x_prompt, x_sample, cache_k, cache_v, cache_kidx, state_hgrn, page_table, hgrn_lb_logits,
           w_in, hgrn_norm_g, w_up_a, w_up_b, w_o, ln1_g, ln1_b, w_ffn_gate, w_ffn_up, w_ffn_down,
           ln2_g, ln2_b):
    depth = w_in.shape[0]
    alpha = (2.0 * depth) ** 0.25
    lb_all = jnp.cumsum(jax.nn.softmax(hgrn_lb_logits.astype(F32), axis=0), axis=0)
    cache_k_t = cache_k.transpose(0, 1, 3, 4, 2)
    cache_v_t = cache_v.transpose(0, 1, 3, 4, 2)
    cache_kidx_t = cache_kidx.transpose(0, 1, 3, 2)
    xp, xs = x_prompt, x_sample
    outs_p, outs_s = [], []
    for l in range(depth):
        pw = _prep_weights(lb_all[l], w_in[l], hgrn_norm_g[l], w_up_a[l], w_up_b[l], w_o[l],
                           ln1_g[l], ln1_b[l], w_ffn_gate[l], w_ffn_up[l], w_ffn_down[l],
                           ln2_g[l], ln2_b[l])
        xp, sp, kp, vp, kip = _prompt_layer(xp, pw, alpha=alpha)
        xs, ss, ks, vs, kis = _sample_layer(xs, state_hgrn[l], cache_k_t, cache_v_t, cache_kidx_t,
                                            page_table, pw, alpha=alpha, layer=l)
        outs_p.append((kp, vp, kip, sp))
        outs_s.append((ks, vs, kis, ss))
    stack = lambda outs, j: jnp.stack([o[j] for o in outs], 0)
    return (xp, xs, stack(outs_p, 0), stack(outs_p, 1), stack(outs_p, 2), stack(outs_p, 3),
            stack(outs_s, 0), stack(outs_s, 1), stack(outs_s, 2), stack(outs_s, 3))
```

```python
import functools

import jax
import jax.numpy as jnp
from jax import lax
from jax.experimental import pallas as pl
from jax.experimental.pallas import tpu as pltpu

F32 = jnp.float32
BF16 = jnp.bfloat16
I32 = jnp.int32

A_HEADS = 4
A_DK = 128
A_DV = 128
B_HEADS = 8
B_HDIM = 64
IDX_HEADS = 8
IDX_DIM = 64
TOPK_MAX = 256
PAGE_SIZE = 128
LN_EPS = 1e-5
RMS_EPS = 1e-6
ATTN_SCALE = B_HDIM ** -0.5
IDX_SCALE = (IDX_DIM * IDX_HEADS) ** -0.5

LANES = 128
SUBLANES = 8
VMEM_LIMIT_BYTES = 56 * 1024 * 1024

HGRN_CHUNK = 64
HGRN_BLOCK = 16
Q_BLOCK = 128
K_BLOCK = 128
K_TILE = 256
PAGE_RING = 32

INT_MIN = -2 ** 31
NEG_INF_KEY = INT_MIN + 0x007FFFFF


def _nt(a, b):
    return lax.dot_general(a, b, (((1,), (1,)), ((), ())), preferred_element_type=F32)


def _tn(a, b):
    return lax.dot_general(a, b, (((0,), (0,)), ((), ())), preferred_element_type=F32)


def _mm(a, b):
    return jnp.dot(a, b, preferred_element_type=F32)


def _const_spec(shape):
    nd = len(shape)
    return pl.BlockSpec(shape, lambda *_: (0,) * nd, pipeline_mode=pl.Buffered(1))


def _params(*sem):
    return pltpu.CompilerParams(dimension_semantics=sem, vmem_limit_bytes=VMEM_LIMIT_BYTES)


def _float_key(x):
    bits = pltpu.bitcast(x, I32)
    return bits ^ ((bits >> 31) & 0x7FFFFFFF)


def _lane_column(x, b):
    lane = lax.broadcasted_iota(I32, x.shape, 1)
    return jnp.sum(jnp.where(lane == b, x, 0.0), axis=1, keepdims=True)


def _split3(x):
    hi = x.astype(BF16)
    r = x - hi.astype(F32)
    mid = r.astype(BF16)
    lo = (r - mid.astype(F32)).astype(BF16)
    return hi, mid, lo


def _sum6(hh, hm, mh, hl, lh, mm):
    return hh + ((hm + mh) + ((hl + lh) + mm))


def _proj_kernel(x_ref, wm_ref, wkvt_ref, wikt_ref, wacc_ref, wiwh_ref, wiwm_ref, wiwl_ref, wg_ref, lb_ref,
                 aq_ref, f_ref, ai_ref, gate_ref, qb_ref, kt_ref, kb_ref, vt_ref, vtb_ref,
                 kit_ref, qia_ref, kia_ref, iwt_ref, ga_ref, gb_ref, *, tm):
    x = x_ref[...]
    xb, xm, xl = _split3(x)
    w = A_HEADS * A_DK

    na = wacc_ref.shape[1] // 3
    t1 = _mm(xb, wacc_ref[...])
    t2 = _mm(xm, wacc_ref[:, :2 * na])
    t3 = _mm(xl, wacc_ref[:, :na])
    acc = _sum6(t1[:, :na], t1[:, na:2 * na], t2[:, :na], t1[:, 2 * na:], t3, t2[:, na:])
    qia_ref[...] = acc[:, :w]
    kia_ref[...] = acc[:, w:]
    wh, wmid, wlo = wiwh_ref[...], wiwm_ref[...], wiwl_ref[...]
    iwt_ref[...] = _sum6(_nt(wh, xb), _nt(wmid, xb), _nt(wh, xm), _nt(wlo, xb), _nt(wh, xl),
                         _nt(wmid, xm))

    def mm(j):
        return _mm(xb, wm_ref[:, j * w:(j + 1) * w])

    aq_ref[...] = mm(0)
    lb = lb_ref[...]
    f_ref[...] = lb + (1.0 - lb) * jax.nn.sigmoid(mm(1))
    ai_ref[...] = mm(2)
    ag = mm(3)
    gate_ref[...] = ag * jax.nn.sigmoid(ag)
    qb_ref[...] = (mm(4) * ATTN_SCALE).astype(BF16)
    kb_ref[...] = mm(5).astype(BF16)
    kvt = _nt(wkvt_ref[...], xb)
    kt_ref[0] = kvt[:w]
    vt = kvt[w:]
    vt_ref[0] = vt
    for r in range(tm // K_BLOCK):
        for p in range(B_HEADS // 2):
            vtb_ref[r, p] = vt[p * LANES:(p + 1) * LANES, r * K_BLOCK:(r + 1) * K_BLOCK].astype(BF16)
    kit_ref[0] = _nt(wikt_ref[...], xb)
    d = ga_ref.shape[1]
    ga_ref[...] = jax.nn.sigmoid(_mm(xb, wg_ref[:, :d]))
    gb_ref[...] = jax.nn.sigmoid(_mm(xb, wg_ref[:, d:]))


def _proj(x, wm, wkvt, wikt, wacc, wiw_h, wiw_m, wiw_l, wg, lb_row, *, batch, seq, tm):
    n, d = x.shape
    w = A_HEADS * A_DK
    nb = n // K_BLOCK
    tps = seq // tm
    row = lambda width: pl.BlockSpec((tm, width), lambda i: (i, 0))
    tmajor = lambda rows: pl.BlockSpec((1, rows, tm), lambda i: (i // tps, 0, i % tps))
    out_shape = (
        jax.ShapeDtypeStruct((n, w), F32),
        jax.ShapeDtypeStruct((n, w), F32),
        jax.ShapeDtypeStruct((n, w), F32),
        jax.ShapeDtypeStruct((n, w), F32),
        jax.ShapeDtypeStruct((n, w), BF16),
        jax.ShapeDtypeStruct((batch, w, seq), F32),
        jax.ShapeDtypeStruct((n, w), BF16),
        jax.ShapeDtypeStruct((batch, w, seq), F32),
        jax.ShapeDtypeStruct((nb, B_HEADS // 2, LANES, K_BLOCK), BF16),
        jax.ShapeDtypeStruct((batch, IDX_DIM, seq), F32),
        jax.ShapeDtypeStruct((n, w), F32),
        jax.ShapeDtypeStruct((n, 2 * IDX_DIM), F32),
        jax.ShapeDtypeStruct((IDX_HEADS, n), F32),
        jax.ShapeDtypeStruct((n, d), F32),
        jax.ShapeDtypeStruct((n, d), F32),
    )
    out_specs = (
        row(w), row(w), row(w), row(w), row(w), tmajor(w), row(w), tmajor(w),
        pl.BlockSpec((tm // K_BLOCK, B_HEADS // 2, LANES, K_BLOCK), lambda i: (i, 0, 0, 0)),
        tmajor(IDX_DIM), row(w), row(2 * IDX_DIM),
        pl.BlockSpec((IDX_HEADS, tm), lambda i: (0, i)),
        row(d), row(d),
    )
    consts = (wm, wkvt, wikt, wacc, wiw_h, wiw_m, wiw_l, wg, lb_row)
    return pl.pallas_call(
        functools.partial(_proj_kernel, tm=tm),
        grid=(n // tm,),
        in_specs=[row(d)] + [_const_spec(a.shape) for a in consts],
        out_specs=out_specs,
        out_shape=out_shape,
        compiler_params=_params("arbitrary"),
        name="proj",
    )(x, *consts)


def _hgrn_kernel(aq_ref, f_ref, ai_ref, gate_ref, gn_ref, o_ref, s_ref,
                 st_ref, *head_scratch):
    C, BS = HGRN_CHUNK, HGRN_BLOCK
    c = pl.program_id(1)
    per_head = [head_scratch[5 * h:5 * h + 5] for h in range(A_HEADS)]

    @pl.when(c == 0)
    def _():
        st_ref[...] = jnp.zeros_like(st_ref)
        for kpad, gpad, vpad, _, _ in per_head:
            kpad[...] = jnp.zeros_like(kpad)
            gpad[...] = jnp.zeros_like(gpad)
            vpad[...] = jnp.zeros_like(vpad)

    ri = lax.broadcasted_iota(I32, (C, C), 0)
    ci = lax.broadcasted_iota(I32, (C, C), 1)
    tri = jnp.where(ci <= ri, 1.0, 0.0).astype(BF16)

    row = lax.broadcasted_iota(I32, (C, A_DK), 0)
    blk = row // BS
    rr = ri // BS
    cc = ci // BS
    mask1 = (rr >= 2) & (cc < 2)
    mask2 = ((rr == 1) & (cc == 0)) | ((rr == 3) & (cc == 2))
    ones_bf = jnp.ones((A_DK, LANES), BF16)
    neg_inf = -jnp.inf

    for h in range(A_HEADS):
        sl = slice(h * A_DK, (h + 1) * A_DK)
        q = aq_ref[:, sl]
        f = f_ref[:, sl]
        v = ai_ref[:, sl]
        g = jnp.log(f)
        k = 1.0 - f
        g_hi = g.astype(BF16)
        r1 = g - g_hi.astype(F32)
        g_mid = r1.astype(BF16)
        g_lo = (r1 - g_mid.astype(F32)).astype(BF16)
        cs = _mm(tri, jnp.concatenate([g_hi, g_mid, g_lo], axis=1))
        G = cs[:, :A_DK] + cs[:, A_DK:2 * A_DK] + cs[:, 2 * A_DK:]
        g15 = G[BS - 1:BS]
        g31 = G[2 * BS - 1:2 * BS]
        g47 = G[3 * BS - 1:3 * BS]
        g_last = G[C - 1:C]

        v_bf = v.astype(BF16)
        st = st_ref[h]
        o = _nt((q * jnp.exp(G)).astype(BF16), st.astype(BF16))

        qe1 = jnp.exp(jnp.where(blk >= 2, G - g31, neg_inf))
        ke1 = jnp.exp(jnp.where(blk < 2, g31 - G, neg_inf))
        ref2 = jnp.where(blk < 2, g15, g47)
        qe2 = jnp.exp(jnp.where((blk == 1) | (blk == 3), G - ref2, neg_inf))
        ke2 = jnp.exp(jnp.where((blk == 0) | (blk == 2), ref2 - G, neg_inf))
        a1 = _nt((q * qe1).astype(BF16), (k * ke1).astype(BF16))
        a2 = _nt((q * qe2).astype(BF16), (k * ke2).astype(BF16))
        a = jnp.where(mask1, a1, 0.0) + jnp.where(mask2, a2, 0.0)
        o = o + _mm(a.astype(BF16), v_bf)

        kpad, gpad, vpad, dstack, rsum_s = per_head[h]
        kpad[BS:, :] = k
        gpad[BS:, :] = G
        vpad[BS:, :] = v
        for d in range(BS):
            ks = kpad[BS - d:BS - d + C, :]
            gs = gpad[BS - d:BS - d + C, :]
            e = jnp.exp(jnp.where((row % BS) >= d, G - gs, neg_inf))
            dstack[d * C:(d + 1) * C, :] = (q * ks * e).astype(BF16)
        rsum_s[...] = _mm(dstack[...], ones_bf)
        for d in range(BS):
            o = o + rsum_s[d * C:(d + 1) * C, :] * vpad[BS - d:BS - d + C, :]

        kt = (k * jnp.exp(g_last - G)).astype(BF16)
        st_new = st * jnp.exp(g_last) + _tn(v_bf, kt)
        st_ref[h] = st_new

        ms = jnp.mean(o * o, axis=-1, keepdims=True)
        on = o * lax.rsqrt(ms + RMS_EPS) * gn_ref[...]
        o_ref[:, sl] = (on * gate_ref[:, sl]).astype(BF16)

    @pl.when(c == pl.num_programs(1) - 1)
    def _():
        for h in range(A_HEADS):
            s_ref[0, h] = st_ref[h].T


def _hgrn_prompt(aq, f, ai, gate, gn_row, *, batch, seq):
    n, w = aq.shape
    nc = seq // HGRN_CHUNK
    blk = pl.BlockSpec((HGRN_CHUNK, w), lambda b, c: (b * nc + c, 0))
    pad_rows = HGRN_BLOCK + HGRN_CHUNK
    return pl.pallas_call(
        _hgrn_kernel,
        grid=(batch, nc),
        in_specs=[blk, blk, blk, blk, _const_spec(gn_row.shape)],
        out_specs=(blk, pl.BlockSpec((1, A_HEADS, A_DK, A_DV), lambda b, c: (b, 0, 0, 0))),
        out_shape=(jax.ShapeDtypeStruct((n, w), BF16),
                   jax.ShapeDtypeStruct((batch, A_HEADS, A_DK, A_DV), F32)),
        scratch_shapes=[pltpu.VMEM((A_HEADS, A_DV, A_DK), F32)] + A_HEADS * [
            pltpu.VMEM((pad_rows, A_DK), F32),
            pltpu.VMEM((pad_rows, A_DK), F32),
            pltpu.VMEM((pad_rows, A_DV), F32),
            pltpu.VMEM((HGRN_BLOCK * HGRN_CHUNK, A_DK), BF16),
            pltpu.VMEM((HGRN_BLOCK * HGRN_CHUNK, LANES), F32)],
        compiler_params=_params("arbitrary", "arbitrary"),
        name="hgrn_prompt",
    )(aq, f, ai, gate, gn_row)


def _count_rows(key_s, nkt, hits, n_out=1):
    def body(kt, accs):
        off = pl.multiple_of(kt * K_TILE, K_TILE)
        hs = hits(key_s[pl.ds(off, K_TILE), :], kt)
        return tuple(a + h.reshape(K_TILE // SUBLANES, SUBLANES, Q_BLOCK).sum(axis=0)
                     for a, h in zip(accs, hs))
    accs = lax.fori_loop(0, nkt, body, tuple(jnp.zeros((SUBLANES, Q_BLOCK), I32) for _ in range(n_out)))
    return tuple(a.sum(axis=0, keepdims=True) for a in accs)


def _swap_halves(x):
    return jnp.concatenate([x[:, B_HDIM:], x[:, :B_HDIM]], axis=1)


def _attn_kernel(q_ref, qia_ref, iwt_ref, k_ref, kia_ref, vt_ref, o_ref,
                 kaug, kst1, kst2, r1_s, r2_s, rs_s, key_s, bias_s, lg_s, ot_s, *, seq, topk):
    i = pl.program_id(1)
    nkt = (i * Q_BLOCK) // K_TILE + 1
    npair = B_HEADS // 2
    lane = lax.broadcasted_iota(I32, (Q_BLOCK, LANES), 1)
    krow = lax.broadcasted_iota(I32, (K_TILE, Q_BLOCK), 0)
    qpos = i * Q_BLOCK + lax.broadcasted_iota(I32, (K_TILE, Q_BLOCK), 1)

    @pl.when(i == 0)
    def _():
        def body(kb, carry):
            off = pl.multiple_of(kb * K_BLOCK, K_BLOCK)
            pos = off + lax.broadcasted_iota(I32, (K_BLOCK, LANES), 0)
            ln = lax.broadcasted_iota(I32, (K_BLOCK, LANES), 1)
            feat = jnp.where(ln == 0, (pos // 64) * 64, jnp.where(ln == 1, pos % 64, 0))
            feat = feat.astype(F32).astype(BF16)
            for p in range(npair):
                kaug[p, pl.ds(off, K_BLOCK), :LANES] = k_ref[pl.ds(off, K_BLOCK), p * LANES:(p + 1) * LANES]
                kaug[p, pl.ds(off, K_BLOCK), LANES:] = feat
            kh, km, kl = _split3(kia_ref[pl.ds(off, K_BLOCK), :])
            half = ln < IDX_DIM
            kst1[pl.ds(off, K_BLOCK), :LANES] = kh
            kst1[pl.ds(off, K_BLOCK), LANES:] = jnp.where(half, km, kh)
            kst2[pl.ds(off, K_BLOCK), :] = jnp.where(half, kl, km)
            return carry
        lax.fori_loop(0, seq // K_BLOCK, body, 0)

    zero_bf = jnp.zeros((Q_BLOCK, LANES), BF16)
    for p in range(npair):
        qp = q_ref[:, p * LANES:(p + 1) * LANES]
        lo = lane < B_HDIM
        qh, qm, ql = _split3(qia_ref[:, p * LANES:(p + 1) * LANES])
        rqh = _swap_halves(qh)
        a0 = jnp.where(lo, qh, _swap_halves(qm))
        b0 = jnp.where(lo, rqh, qm)
        r1_s[p, :Q_BLOCK, :LANES] = a0
        r1_s[p, :Q_BLOCK, LANES:] = jnp.where(lo, qh, _swap_halves(ql))
        r1_s[p, Q_BLOCK:, :LANES] = b0
        r1_s[p, Q_BLOCK:, LANES:] = jnp.where(lo, rqh, ql)
        r2_s[p, :Q_BLOCK, :] = a0
        r2_s[p, Q_BLOCK:, :] = b0
        rs_s[p, :Q_BLOCK, :LANES] = jnp.where(lo, qp, zero_bf)
        rs_s[p, Q_BLOCK:, :LANES] = jnp.where(lo, zero_bf, qp)
        m0 = 2.0 ** -(2 * p + 1)
        m1 = 2.0 ** -(2 * p + 2)
        rs_s[p, :Q_BLOCK, LANES:] = jnp.where(lane < 2, m0, 0.0).astype(BF16)
        rs_s[p, Q_BLOCK:, LANES:] = jnp.where(lane < 2, m1, 0.0).astype(BF16)

    def score_body(kt, carry):
        off = pl.multiple_of(kt * K_TILE, K_TILE)
        k1 = kst1[pl.ds(off, K_TILE), :]
        k2 = kst2[pl.ds(off, K_TILE), :]
        acc = jnp.zeros((K_TILE, Q_BLOCK), F32)
        for p in range(npair):
            s2 = _nt(k1, r1_s[p]) + _nt(k2, r2_s[p])
            for hh in range(2):
                hd = 2 * p + hh
                acc = acc + iwt_ref[hd:hd + 1, :] * jnp.maximum(s2[:, hh * Q_BLOCK:(hh + 1) * Q_BLOCK], 0.0)
        sc = jnp.where(off + krow <= qpos, acc * IDX_SCALE, -jnp.inf)
        key_s[pl.ds(off, K_TILE), :] = _float_key(sc)
        return carry
    lax.fori_loop(0, nkt, score_body, 0)

    def bit_body(it, ans):
        cand = ans | lax.shift_left(jnp.int32(1), 31 - it)
        cnt, = _count_rows(key_s, nkt, lambda blk, kt: (jnp.where(blk >= (cand ^ INT_MIN), 1, 0),))
        return jnp.where(cnt >= topk, cand, ans)
    ans = lax.fori_loop(0, 32, bit_body, jnp.zeros((1, Q_BLOCK), I32))
    thr = ans ^ INT_MIN
    thr_gt = jnp.maximum(thr, NEG_INF_KEY)
    n_gt, n_eq = _count_rows(
        key_s, nkt, lambda blk, kt: (jnp.where(blk > thr_gt, 1, 0), jnp.where(blk == thr, 1, 0)), n_out=2)
    need = topk - n_gt

    def tie_search():
        def tie_body(it, y):
            cand = y | lax.shift_left(jnp.int32(1), 11 - it)
            cnt, = _count_rows(
                key_s, nkt,
                lambda blk, kt: (jnp.where(blk == thr, jnp.where(kt * K_TILE + krow < cand, 1, 0), 0),))
            return jnp.where(cnt < need, cand, y)
        return lax.fori_loop(0, 12, tie_body, jnp.zeros((1, Q_BLOCK), I32))
    surplus = jnp.max(jnp.where(thr > NEG_INF_KEY, n_eq - need, 0))
    y = lax.cond(surplus > 0, tie_search, lambda: jnp.full((1, Q_BLOCK), 4095, I32))
    ylim = jnp.where(thr <= NEG_INF_KEY, 0, y + 1)

    def bias_body(kt, carry):
        off = pl.multiple_of(kt * K_TILE, K_TILE)
        blk = key_s[pl.ds(off, K_TILE), :]
        tie = jnp.where(blk == thr, jnp.where(off + krow < ylim, 0.0, -jnp.inf), -jnp.inf)
        bias_s[pl.ds(off, K_TILE), :] = jnp.where(blk > thr_gt, 0.0, tie)
        return carry
    lax.fori_loop(0, nkt, bias_body, 0)

    def fold(x, op):
        return op(x.reshape(x.shape[0] // SUBLANES, SUBLANES, 2 * Q_BLOCK), axis=0)

    def pass1(kt, mxs):
        off = pl.multiple_of(kt * K_TILE, K_TILE)
        b = bias_s[pl.ds(off, K_TILE), :]
        b2 = jnp.concatenate([b, b], axis=1)
        out = []
        for p in range(npair):
            lg = _nt(kaug[p, pl.ds(off, K_TILE), :], rs_s[p]) + b2
            lg_s[p, pl.ds(off, K_TILE), :] = lg
            out.append(jnp.maximum(mxs[p], fold(lg, jnp.max)))
        return tuple(out)
    mxs = lax.fori_loop(0, nkt, pass1, tuple(jnp.full((SUBLANES, 2 * Q_BLOCK), -jnp.inf, F32)
                                             for _ in range(npair)))
    ms = [mx.max(axis=0, keepdims=True) for mx in mxs]

    ot_s[...] = jnp.zeros_like(ot_s)

    def pass2(kt, l8s):
        out = list(l8s)
        for u in range(K_TILE // K_BLOCK):
            kb = kt * (K_TILE // K_BLOCK) + u
            off = pl.multiple_of(kb * K_BLOCK, K_BLOCK)
            for p in range(npair):
                pe = jnp.exp(lg_s[p, pl.ds(off, K_BLOCK), :] - ms[p])
                ot_s[p] += _mm(vt_ref[kb, p], pe.astype(BF16))
                out[p] = out[p] + fold(pe, jnp.sum)
        return tuple(out)
    l8s = lax.fori_loop(0, nkt, pass2, tuple(jnp.zeros((SUBLANES, 2 * Q_BLOCK), F32) for _ in range(npair)))
    for p in range(npair):
        inv = 1.0 / l8s[p].sum(axis=0, keepdims=True)
        ot = ot_s[p]
        top = (ot[:B_HDIM, :Q_BLOCK] * inv[:, :Q_BLOCK]).T
        bot = (ot[B_HDIM:, Q_BLOCK:] * inv[:, Q_BLOCK:]).T
        o_ref[:, p * LANES:(p + 1) * LANES] = jnp.concatenate([top, bot], axis=1).astype(BF16)


def _attn_prompt(qb, qia, iwt, kb, kia, vtb, *, batch, seq, topk):
    n, w = qb.shape
    nq = seq // Q_BLOCK
    npair = B_HEADS // 2
    assert seq % K_TILE == 0 and K_TILE % K_BLOCK == 0
    qspec = pl.BlockSpec((Q_BLOCK, w), lambda b, i: (b * nq + i, 0))
    return pl.pallas_call(
        functools.partial(_attn_kernel, seq=seq, topk=topk),
        grid=(batch, nq),
        in_specs=[qspec, qspec,
                  pl.BlockSpec((IDX_HEADS, Q_BLOCK), lambda b, i: (0, b * nq + i)),
                  pl.BlockSpec((seq, w), lambda b, i: (b, 0)),
                  pl.BlockSpec((seq, 2 * IDX_DIM), lambda b, i: (b, 0)),
                  pl.BlockSpec((seq // K_BLOCK, npair, LANES, K_BLOCK), lambda b, i: (b, 0, 0, 0))],
        out_specs=qspec,
        out_shape=jax.ShapeDtypeStruct((n, w), BF16),
        scratch_shapes=[pltpu.VMEM((npair, seq, 2 * LANES), BF16),
                        pltpu.VMEM((seq, 2 * LANES), BF16),
                        pltpu.VMEM((seq, LANES), BF16),
                        pltpu.VMEM((npair, 2 * Q_BLOCK, 2 * LANES), BF16),
                        pltpu.VMEM((npair, 2 * Q_BLOCK, LANES), BF16),
                        pltpu.VMEM((npair, 2 * Q_BLOCK, 2 * LANES), BF16),
                        pltpu.VMEM((seq, Q_BLOCK), I32),
                        pltpu.VMEM((seq, Q_BLOCK), F32),
                        pltpu.VMEM((npair, seq, 2 * Q_BLOCK), F32),
                        pltpu.VMEM((npair, LANES, 2 * Q_BLOCK), F32)],
        compiler_params=_params("arbitrary", "arbitrary"),
        name="attn_prompt",
    )(qb, qia, iwt, kb, kia, vtb)


def _layer_norm(x, g, b):
    mu = jnp.mean(x, axis=-1, keepdims=True)
    xc = x - mu
    var = jnp.mean(xc * xc, axis=-1, keepdims=True)
    return xc * lax.rsqrt(var + LN_EPS) * g + b


def _final_kernel(x_ref, oa_ref, ob_ref, ga_ref, gb_ref, wua_ref, wub_ref, wo_ref,
                  l1g_ref, l1b_ref, wfg_ref, wfu_ref, wfd_ref, l2g_ref, l2b_ref, y_ref,
                  *, alpha, ff_chunk):
    ya = _mm(oa_ref[...], wua_ref[...])
    yb = _mm(ob_ref[...], wub_ref[...])
    merged = (ga_ref[...] * ya + gb_ref[...] * yb).astype(BF16)
    mix = _mm(merged, wo_ref[...])
    x1 = _layer_norm(alpha * x_ref[...] + mix, l1g_ref[...], l1b_ref[...])
    x1b = x1.astype(BF16)
    d_ff = wfg_ref.shape[1]
    acc = jnp.zeros(x1.shape, F32)
    for c in range(d_ff // ff_chunk):
        cs = slice(c * ff_chunk, (c + 1) * ff_chunk)
        hg = _mm(x1b, wfg_ref[:, cs])
        hu = _mm(x1b, wfu_ref[:, cs])
        hc = (hg * jax.nn.sigmoid(hg) * hu).astype(BF16)
        acc = acc + _mm(hc, wfd_ref[cs, :])
    y_ref[...] = _layer_norm(alpha * x1 + acc, l2g_ref[...], l2b_ref[...])


def _final(x, oa, ob, ga, gb, wua, wub, wo, l1g, l1b, wfg, wfu, wfd, l2g, l2b, *, tm, alpha):
    n, d = x.shape
    w = oa.shape[1]
    row = lambda width: pl.BlockSpec((tm, width), lambda i: (i, 0))
    consts = (wua, wub, wo, l1g, l1b, wfg, wfu, wfd, l2g, l2b)
    return pl.pallas_call(
        functools.partial(_final_kernel, alpha=alpha, ff_chunk=256),
        grid=(n // tm,),
        in_specs=[row(d), row(w), row(w), row(d), row(d)] + [_const_spec(a.shape) for a in consts],
        out_specs=row(d),
        out_shape=jax.ShapeDtypeStruct((n, d), F32),
        compiler_params=_params("arbitrary"),
        name="merge_ffn",
    )(x, oa, ob, ga, gb, *consts)


def _hgrn_step_kernel(qt_ref, ft_ref, v_ref, gate_ref, gn_ref, s_ref, so_ref, o_ref, *, per_step):
    for j in range(per_step):
        b = pl.program_id(0) * per_step + j
        qcol = _lane_column(qt_ref[...], b)
        fcol = _lane_column(ft_ref[...], b)
        for h in range(A_HEADS):
            sl = slice(h * A_DK, (h + 1) * A_DK)
            fc = fcol[sl]
            vrow = v_ref[j, :, sl]
            s_new = fc * s_ref[j, h] + (1.0 - fc) * vrow
            so_ref[j, h] = s_new
            o = jnp.sum(qcol[sl] * s_new, axis=0, keepdims=True)
            ms = jnp.mean(o * o, axis=-1, keepdims=True)
            on = o * lax.rsqrt(ms + RMS_EPS) * gn_ref[...]
            o_ref[j, :, sl] = (on * gate_ref[j, :, sl]).astype(BF16)


def _hgrn_step(aq_t, f_t, ai3, gate3, gn_row, state):
    nb = state.shape[0]
    w = A_HEADS * A_DK
    per_step = 4 if nb % 4 == 0 else 1
    sspec = pl.BlockSpec((per_step, A_HEADS, A_DK, A_DV), lambda b: (b, 0, 0, 0))
    rspec = pl.BlockSpec((per_step, 1, w), lambda b: (b, 0, 0))
    return pl.pallas_call(
        functools.partial(_hgrn_step_kernel, per_step=per_step),
        grid=(nb // per_step,),
        in_specs=[_const_spec(aq_t.shape), _const_spec(f_t.shape), rspec, rspec,
                  _const_spec(gn_row.shape), sspec],
        out_specs=(sspec, rspec),
        out_shape=(jax.ShapeDtypeStruct(state.shape, F32),
                   jax.ShapeDtypeStruct((nb, 1, w), BF16)),
        compiler_params=_params("arbitrary"),
        name="hgrn_step",
    )(aq_t, f_t, ai3, gate3, gn_row, state)


def _page_copies(pt_ref, cache_ref, buf, sem, b, slot, n_pages, layer):
    def body(pg, carry):
        pltpu.make_async_copy(cache_ref.at[layer, pt_ref[b, pg]], buf.at[slot, pg], sem.at[slot]).start()
        return carry
    lax.fori_loop(0, n_pages, body, 0)


def _idx_score_kernel(pt_ref, qi_ref, iwt_ref, kn_ref, cache_ref, sc_ref, sn_ref, buf, sem,
                      *, n_pages, layer):
    b = pl.program_id(0)
    nb = pl.num_programs(0)
    slot = b % 2

    @pl.when(b == 0)
    def _():
        _page_copies(pt_ref, cache_ref, buf, sem, b, slot, n_pages, layer)

    @pl.when(b + 1 < nb)
    def _():
        _page_copies(pt_ref, cache_ref, buf, sem, b + 1, 1 - slot, n_pages, layer)

    qi = qi_ref[0]
    wcol = _lane_column(iwt_ref[...], b)

    def head_sum(s):
        return jnp.sum(wcol * jnp.maximum(s, 0.0), axis=0, keepdims=True) * IDX_SCALE

    sn = jnp.sum(qi * kn_ref[0][:, :IDX_DIM], axis=1, keepdims=True)
    sn_ref[0] = jnp.broadcast_to(head_sum(sn), (1, LANES))

    qh, qm, ql = _split3(qi)
    lhs1 = jnp.concatenate([qh, qm, qh, ql], axis=1)
    lhs2 = jnp.concatenate([qh, qm], axis=1)

    def page_scores(tile):
        kh, km, kl = _split3(tile)
        s = _mm(lhs1, jnp.concatenate([kh, kh, km, kh], axis=0))
        return head_sum(s + _mm(lhs2, jnp.concatenate([kl, km], axis=0)))

    pltpu.make_async_copy(buf.at[slot], buf.at[slot], sem.at[slot]).wait()
    for g in range(n_pages // SUBLANES):
        rows = [page_scores(buf[slot, g * SUBLANES + r]) for r in range(SUBLANES)]
        sc_ref[0, g * SUBLANES:(g + 1) * SUBLANES, :] = jnp.concatenate(rows, axis=0)


def _idx_scores(page_table, qi3, iwt, kidx_new3, cache_kidx_t, *, layer):
    nb, n_pages = page_table.shape
    assert n_pages % SUBLANES == 0
    grid_spec = pltpu.PrefetchScalarGridSpec(
        num_scalar_prefetch=1,
        grid=(nb,),
        in_specs=[pl.BlockSpec((1, IDX_HEADS, IDX_DIM), lambda b, pt: (b, 0, 0)),
                  _const_spec(iwt.shape),
                  pl.BlockSpec((1, 1, 2 * IDX_DIM), lambda b, pt: (b, 0, 0)),
                  pl.BlockSpec(memory_space=pl.ANY)],
        out_specs=(pl.BlockSpec((1, n_pages, PAGE_SIZE), lambda b, pt: (b, 0, 0)),
                   pl.BlockSpec((1, 1, LANES), lambda b, pt: (b, 0, 0))),
        scratch_shapes=[pltpu.VMEM((2, n_pages, IDX_DIM, PAGE_SIZE), F32),
                        pltpu.SemaphoreType.DMA((2,))],
    )
    return pl.pallas_call(
        functools.partial(_idx_score_kernel, n_pages=n_pages, layer=layer),
        grid_spec=grid_spec,
        out_shape=(jax.ShapeDtypeStruct((nb, n_pages, PAGE_SIZE), F32),
                   jax.ShapeDtypeStruct((nb, 1, LANES), F32)),
        compiler_params=_params("arbitrary"),
        name="idx_scores",
    )(page_table, qi3, iwt, kidx_new3, cache_kidx_t)


def _select_kernel(sc_ref, sn_ref, mask_ref, newsel_ref, key_s, *, topk, pos_bits):
    nb, past = sc_ref.shape
    ncol = past // LANES
    key_s[...] = _float_key(sc_ref[...])
    kn = _float_key(sn_ref[...])
    pos = lax.broadcasted_iota(I32, (nb, LANES), 1)

    def count(hits_past, hit_new):
        def body(j, acc):
            off = pl.multiple_of(j * LANES, LANES)
            return acc + hits_past(key_s[:, pl.ds(off, LANES)], off)
        acc = lax.fori_loop(0, ncol, body, jnp.zeros((nb, LANES), I32))
        tot = jnp.sum(acc, axis=1, keepdims=True)
        return jnp.broadcast_to(tot, (nb, LANES)) + hit_new

    def bit_body(it, ans):
        cand = ans | lax.shift_left(jnp.int32(1), 31 - it)
        c = cand ^ INT_MIN
        cnt = count(lambda blk, off: jnp.where(blk >= c, 1, 0), jnp.where(kn >= c, 1, 0))
        return jnp.where(cnt >= topk, cand, ans)
    ans = lax.fori_loop(0, 32, bit_body, jnp.zeros((nb, LANES), I32))
    thr = ans ^ INT_MIN
    thr_gt = jnp.maximum(thr, NEG_INF_KEY)
    need = topk - count(lambda blk, off: jnp.where(blk > thr_gt, 1, 0), jnp.where(kn > thr_gt, 1, 0))
    n_eq = count(lambda blk, off: jnp.where(blk == thr, 1, 0), jnp.where(kn == thr, 1, 0))

    def tie_search():
        def tie_body(it, y):
            cand = y | lax.shift_left(jnp.int32(1), pos_bits - 1 - it)
            cnt = count(lambda blk, off: jnp.where(blk == thr, jnp.where(off + pos < cand, 1, 0), 0),
                        jnp.where(kn == thr, jnp.where(past < cand, 1, 0), 0))
            return jnp.where(cnt < need, cand, y)
        return lax.fori_loop(0, pos_bits, tie_body, jnp.zeros((nb, LANES), I32))
    surplus = jnp.max(jnp.where(thr > NEG_INF_KEY, n_eq - need, 0))
    y = lax.cond(surplus > 0, tie_search, lambda: jnp.full((nb, LANES), 2 ** pos_bits - 1, I32))
    ylim = jnp.where(thr <= NEG_INF_KEY, 0, y + 1)

    def out_body(j, carry):
        off = pl.multiple_of(j * LANES, LANES)
        blk = key_s[:, pl.ds(off, LANES)]
        tie = jnp.where(blk == thr, jnp.where(off + pos < ylim, 1.0, 0.0), 0.0)
        mask_ref[:, pl.ds(off, LANES)] = jnp.where(blk > thr_gt, 1.0, tie)
        return carry
    lax.fori_loop(0, ncol, out_body, 0)
    tie_new = jnp.where(kn == thr, jnp.where(past < ylim, 1, 0), 0)
    newsel_ref[...] = jnp.where(kn > thr_gt, 1, tie_new)


def _select(scores, snew, *, topk):
    nb, past = scores.shape
    pos_bits = (past + 1).bit_length()
    return pl.pallas_call(
        functools.partial(_select_kernel, topk=topk, pos_bits=pos_bits),
        out_shape=(jax.ShapeDtypeStruct((nb, past), F32),
                   jax.ShapeDtypeStruct((nb, LANES), I32)),
        scratch_shapes=[pltpu.VMEM((nb, past), I32)],
        compiler_params=pltpu.CompilerParams(vmem_limit_bytes=VMEM_LIMIT_BYTES),
        name="topk_select",
    )(scores, snew)


def _paged_attn_kernel(pt_ref, qt_ref, knt_ref, vnt_ref, mask_ref, nsel_ref, ck_ref, cv_ref, ot_ref,
                       kring, vring, sem, qcol_s, p_s, acc_s, *, n_pages, ring, layer):
    b = pl.program_id(0)
    nb = pl.num_programs(0)
    past = n_pages * PAGE_SIZE

    def page_copy(cache_ref, buf, which, bb, pg):
        return pltpu.make_async_copy(cache_ref.at[layer, pt_ref[bb, pg]], buf.at[pg % ring],
                                     sem.at[which, pg % ring])

    def start_ahead(cache_ref, buf, which, pg):
        n = pg + ring
        wrap = n >= n_pages
        bb = jnp.where(wrap, b + 1, b)
        pg2 = jnp.where(wrap, n - n_pages, n)

        @pl.when(bb < nb)
        def _():
            page_copy(cache_ref, buf, which, bb, pg2).start()

    @pl.when(b == 0)
    def _():
        ot_ref[...] = jnp.zeros_like(ot_ref)
        for r in range(ring):
            page_copy(ck_ref, kring, 0, 0, r).start()
            page_copy(cv_ref, vring, 1, 0, r).start()

    qcol = _lane_column(qt_ref[...], b)
    kncol = _lane_column(knt_ref[...], b)
    vncol = _lane_column(vnt_ref[...], b)
    for h in range(B_HEADS):
        qcol_s[h] = jnp.broadcast_to(qcol[h * B_HDIM:(h + 1) * B_HDIM], (B_HDIM, PAGE_SIZE))

    sub = lax.broadcasted_iota(I32, (B_HEADS, PAGE_SIZE), 0)
    lane = lax.broadcasted_iota(I32, (B_HEADS, PAGE_SIZE), 1)
    slopes = jnp.exp2(-(sub + 1).astype(F32))

    def head_rows(tile_of_head):
        out = jnp.zeros((B_HEADS, PAGE_SIZE), F32)
        for h in range(B_HEADS):
            out = jnp.where(sub == h, jnp.sum(tile_of_head(h), axis=0, keepdims=True), out)
        return out

    def k_body(pg, mx):
        page_copy(ck_ref, kring, 0, b, pg).wait()
        lg = head_rows(lambda h: kring[pg % ring, h] * qcol_s[h])
        dist = (past - pg * PAGE_SIZE - lane).astype(F32)
        lg = jnp.where(mask_ref[0, pl.ds(pg, 1), :] > 0.0, lg - slopes * dist, -jnp.inf)
        p_s[pg] = lg
        start_ahead(ck_ref, kring, 0, pg)
        return jnp.maximum(mx, lg)
    mx = lax.fori_loop(0, n_pages, k_body, jnp.full((B_HEADS, PAGE_SIZE), -jnp.inf, F32))

    lgn = head_rows(lambda h: qcol_s[h] * kncol[h * B_HDIM:(h + 1) * B_HDIM])
    lgn = jnp.where(nsel_ref[pl.ds(b, 1), :] > 0, lgn, -jnp.inf)
    m = jnp.maximum(jnp.max(mx, axis=1, keepdims=True), lgn)
    pn = jnp.exp(lgn - m)

    def e_body(pg, l):
        pe = jnp.exp(p_s[pg] - m)
        p_s[pg] = pe
        return l + pe
    lsum = lax.fori_loop(0, n_pages, e_body, jnp.zeros((B_HEADS, PAGE_SIZE), F32))
    inv = 1.0 / (jnp.sum(lsum, axis=1, keepdims=True) + pn)

    acc_s[...] = jnp.zeros_like(acc_s)

    def v_body(pg, carry):
        page_copy(cv_ref, vring, 1, b, pg).wait()
        pe = p_s[pg]
        for h in range(B_HEADS):
            acc_s[h] += vring[pg % ring, h] * pe[h:h + 1, :]
        start_ahead(cv_ref, vring, 1, pg)
        return carry
    lax.fori_loop(0, n_pages, v_body, 0)

    lane_o = lax.broadcasted_iota(I32, (B_HDIM, ot_ref.shape[1]), 1)
    for h in range(B_HEADS):
        hs = slice(h * B_HDIM, (h + 1) * B_HDIM)
        oc = jnp.sum(acc_s[h], axis=1, keepdims=True) + pn[h:h + 1, :1] * vncol[hs]
        oc = oc * inv[h:h + 1, :1]
        ot_ref[hs, :] = jnp.where(lane_o == b, oc, ot_ref[hs, :])


def _paged_attn(page_table, q_t, kn_t, vn_t, mask3, newsel, cache_k_t, cache_v_t, *, layer):
    nb, n_pages = page_table.shape
    ring = min(PAGE_RING, n_pages)
    assert n_pages % ring == 0
    w = B_HEADS * B_HDIM
    page = (B_HEADS, B_HDIM, PAGE_SIZE)
    grid_spec = pltpu.PrefetchScalarGridSpec(
        num_scalar_prefetch=1,
        grid=(nb,),
        in_specs=[_const_spec(q_t.shape), _const_spec(kn_t.shape), _const_spec(vn_t.shape),
                  pl.BlockSpec((1, n_pages, PAGE_SIZE), lambda b, pt: (b, 0, 0)),
                  _const_spec(newsel.shape),
                  pl.BlockSpec(memory_space=pl.ANY),
                  pl.BlockSpec(memory_space=pl.ANY)],
        out_specs=pl.BlockSpec((w, nb), lambda b, pt: (0, 0)),
        scratch_shapes=[pltpu.VMEM((ring,) + page, F32),
                        pltpu.VMEM((ring,) + page, F32),
                        pltpu.SemaphoreType.DMA((2, ring)),
                        pltpu.VMEM(page, F32),
                        pltpu.VMEM((n_pages, B_HEADS, PAGE_SIZE), F32),
                        pltpu.VMEM(page, F32)],
    )
    return pl.pallas_call(
        functools.partial(_paged_attn_kernel, n_pages=n_pages, ring=ring, layer=layer),
        grid_spec=grid_spec,
        out_shape=jax.ShapeDtypeStruct((w, nb), F32),
        compiler_params=_params("arbitrary"),
        name="paged_attn",
    )(page_table, q_t, kn_t, vn_t, mask3, newsel, cache_k_t, cache_v_t)


def _prep_weights(lb, w_in, hgrn_norm_g, w_up_a, w_up_b, w_o, ln1_g, ln1_b,
                  w_ffn_gate, w_ffn_up, w_ffn_down, ln2_g, ln2_b):
    w = A_HEADS * A_DK
    o_k, o_v, o_iq = 5 * w, 6 * w, 7 * w
    o_ik = 8 * w
    o_iw = o_ik + IDX_DIM
    o_g = o_iw + IDX_HEADS
    wb = w_in.astype(BF16)
    wt = w_in.T.astype(BF16)
    row = lambda a: a.reshape(1, -1).astype(F32)

    def split3(a):
        hi = a.astype(BF16)
        r = a - hi.astype(F32)
        mid = r.astype(BF16)
        return hi, mid, (r - mid.astype(F32)).astype(BF16)

    w_idx = jnp.concatenate([w_in[:, o_iq:o_ik], w_in[:, o_ik:o_iw], w_in[:, o_ik:o_iw]], axis=1)
    wiw_h, wiw_m, wiw_l = split3(w_in[:, o_iw:o_g].T)
    return dict(
        wm=wb[:, :o_v],
        wkvt=wt[o_k:o_v + w],
        wikt=wt[o_ik:o_iw],
        wacc=jnp.concatenate(split3(w_idx), axis=1),
        wiw_h=wiw_h, wiw_m=wiw_m, wiw_l=wiw_l,
        wg=wb[:, o_g:],
        lb=row(lb),
        gn=row(hgrn_norm_g),
        wua=w_up_a.astype(BF16), wub=w_up_b.astype(BF16), wo=w_o.astype(BF16),
        l1g=row(ln1_g), l1b=row(ln1_b),
        wfg=w_ffn_gate.astype(BF16), wfu=w_ffn_up.astype(BF16), wfd=w_ffn_down.astype(BF16),
        l2g=row(ln2_g), l2b=row(ln2_b),
    )


def _proj_call(x2, pw, *, batch, seq, tm):
    return _proj(x2, pw["wm"], pw["wkvt"], pw["wikt"], pw["wacc"], pw["wiw_h"], pw["wiw_m"], pw["wiw_l"],
                 pw["wg"], pw["lb"], batch=batch, seq=seq, tm=tm)


def _final_call(x2, oa, ob, ga, gb, pw, *, tm, alpha):
    return _final(x2, oa, ob, ga, gb, pw["wua"], pw["wub"], pw["wo"], pw["l1g"], pw["l1b"],
                  pw["wfg"], pw["wfu"], pw["wfd"], pw["l2g"], pw["l2b"], tm=tm, alpha=alpha)


def _token_major(feat_t, batch, seq, heads):
    if heads is None:
        return feat_t.transpose(0, 2, 1)
    return feat_t.reshape(batch, heads, -1, seq).transpose(0, 3, 1, 2)


def _prompt_layer(x, pw, *, alpha):
    batch, seq, d = x.shape
    n = batch * seq
    x2 = x.reshape(n, d)
    (aq, f, ai, gate, qb, kt, kb, vt, vtb, kit, qia, kia, iwt, ga, gb) = _proj_call(
        x2, pw, batch=batch, seq=seq, tm=256)
    oa, s_new = _hgrn_prompt(aq, f, ai, gate, pw["gn"], batch=batch, seq=seq)
    ob = _attn_prompt(qb, qia, iwt, kb, kia, vtb, batch=batch, seq=seq, topk=min(TOPK_MAX, seq // 4))
    y = _final_call(x2, oa, ob, ga, gb, pw, tm=512, alpha=alpha)
    return (y.reshape(batch, seq, d), s_new,
            _token_major(kt, batch, seq, B_HEADS), _token_major(vt, batch, seq, B_HEADS),
            _token_major(kit, batch, seq, None))


def _sample_layer(x, state, cache_k_t, cache_v_t, cache_kidx_t, page_table, pw, *, alpha, layer):
    nb, tn, d = x.shape
    n_pages = page_table.shape[1]
    past = n_pages * PAGE_SIZE
    topk = min(TOPK_MAX, (past + tn) // 4)
    w = A_HEADS * A_DK
    x2 = x.reshape(nb, d)
    (aq, f, ai, gate, qb, kt, kb, vt, vtb, kit, qia, kia, iwt, ga, gb) = _proj_call(
        x2, pw, batch=1, seq=nb, tm=nb)
    s_new, oa3 = _hgrn_step(aq.T, f.T, ai.reshape(nb, 1, w), gate.reshape(nb, 1, w), pw["gn"], state)
    scores, snew = _idx_scores(page_table, qia.reshape(nb, IDX_HEADS, IDX_DIM), iwt,
                               kia.reshape(nb, 1, 2 * IDX_DIM), cache_kidx_t, layer=layer)
    mask, newsel = _select(scores.reshape(nb, past), snew.reshape(nb, LANES), topk=topk)
    ob_t = _paged_attn(page_table, qb.astype(F32).T, kt[0], vt[0], mask.reshape(nb, n_pages, PAGE_SIZE),
                       newsel, cache_k_t, cache_v_t, layer=layer)
    y = _final_call(x2, oa3.reshape(nb, w), ob_t.T.astype(BF16), ga, gb, pw, tm=nb, alpha=alpha)
    tok = lambda a, heads: _token_major(a, 1, nb, heads)[0][:, None]
    return (y.reshape(nb, tn, d), s_new, tok(kt, B_HEADS), tok(vt, B_HEADS), tok(kit, None))


def kernel(x_prompt, x_sample, cache_k, cache_v, cache_kidx, state_hgrn, page_table, hgrn_lb_logits,
           w_in, hgrn_norm_g, w_up_a, w_up_b, w_o, ln1_g, ln1_b, w_ffn_gate, w_ffn_up, w_ffn_down,
           ln2_g, ln2_b):
    depth = w_in.shape[0]
    alpha = (2.0 * depth) ** 0.25
    lb_all = jnp.cumsum(jax.nn.softmax(hgrn_lb_logits.astype(F32), axis=0), axis=0)
    cache_k_t = cache_k.transpose(0, 1, 3, 4, 2)
    cache_v_t = cache_v.transpose(0, 1, 3, 4, 2)
    cache_kidx_t = cache_kidx.transpose(0, 1, 3, 2)
    xp, xs = x_prompt, x_sample
    outs_p, outs_s = [], []
    for l in range(depth):
        pw = _prep_weights(lb_all[l], w_in[l], hgrn_norm_g[l], w_up_a[l], w_up_b[l], w_o[l],
                           ln1_g[l], ln1_b[l], w_ffn_gate[l], w_ffn_up[l], w_ffn_down[l],
                           ln2_g[l], ln2_b[l])
        xp, sp, kp, vp, kip = _prompt_layer(xp, pw, alpha=alpha)
        xs, ss, ks, vs, kis = _sample_layer(xs, state_hgrn[l], cache_k_t, cache_v_t, cache_kidx_t,
                                            page_table, pw, alpha=alpha, layer=l)
        outs_p.append((kp, vp, kip, sp))
        outs_s.append((ks, vs, kis, ss))
    stack = lambda outs, j: jnp.stack([o[j] for o in outs], 0)
    return (xp, xs, stack(outs_p, 0), stack(outs_p, 1), stack(outs_p, 2), stack(outs_p, 3),
            stack(outs_s, 0), stack(outs_s, 1), stack(outs_s, 2), stack(outs_s, 3))
```

```python
import functools

import jax
import jax.numpy as jnp
from jax import lax
from jax.experimental import pallas as pl
from jax.experimental.pallas import tpu as pltpu

F32 = jnp.float32
BF16 = jnp.bfloat16
I32 = jnp.int32

A_HEADS = 4
A_DK = 128
A_DV = 128
B_HEADS = 8
B_HDIM = 64
IDX_HEADS = 8
IDX_DIM = 64
TOPK_MAX = 256
PAGE_SIZE = 128
LN_EPS = 1e-5
RMS_EPS = 1e-6
ATTN_SCALE = B_HDIM ** -0.5
IDX_SCALE = (IDX_DIM * IDX_HEADS) ** -0.5

LANES = 128
SUBLANES = 8
VMEM_LIMIT_BYTES = 56 * 1024 * 1024

HGRN_CHUNK = 64
HGRN_BLOCK = 16
Q_BLOCK = 128
K_BLOCK = 128
K_TILE = 256
PAGE_RING = 64

INT_MIN = -2 ** 31
NEG_INF_KEY = INT_MIN + 0x007FFFFF


def _nt(a, b):
    return lax.dot_general(a, b, (((1,), (1,)), ((), ())), preferred_element_type=F32)


def _tn(a, b):
    return lax.dot_general(a, b, (((0,), (0,)), ((), ())), preferred_element_type=F32)


def _mm(a, b):
    return jnp.dot(a, b, preferred_element_type=F32)


def _const_spec(shape):
    nd = len(shape)
    return pl.BlockSpec(shape, lambda *_: (0,) * nd, pipeline_mode=pl.Buffered(1))


def _params(*sem):
    return pltpu.CompilerParams(dimension_semantics=sem, vmem_limit_bytes=VMEM_LIMIT_BYTES)


def _float_key(x):
    bits = pltpu.bitcast(x, I32)
    return bits ^ ((bits >> 31) & 0x7FFFFFFF)


def _lane_column(x, b):
    lane = lax.broadcasted_iota(I32, x.shape, 1)
    return jnp.sum(jnp.where(lane == b, x, 0.0), axis=1, keepdims=True)


def _split3(x):
    hi = x.astype(BF16)
    r = x - hi.astype(F32)
    mid = r.astype(BF16)
    lo = (r - mid.astype(F32)).astype(BF16)
    return hi, mid, lo


def _sum6(hh, hm, mh, hl, lh, mm):
    return hh + ((hm + mh) + ((hl + lh) + mm))


def _proj_kernel(x_ref, wm_ref, wkvt_ref, wikt_ref, wacc_ref, wiwh_ref, wiwm_ref, wiwl_ref, wg_ref, lb_ref,
                 aq_ref, f_ref, ai_ref, gate_ref, qb_ref, kt_ref, kb_ref, vt_ref, vtb_ref,
                 kit_ref, qia_ref, kia_ref, iwt_ref, ga_ref, gb_ref, *, tm):
    x = x_ref[...]
    xb, xm, xl = _split3(x)
    w = A_HEADS * A_DK

    na = wacc_ref.shape[1] // 3
    t1 = _mm(xb, wacc_ref[...])
    t2 = _mm(xm, wacc_ref[:, :2 * na])
    t3 = _mm(xl, wacc_ref[:, :na])
    acc = _sum6(t1[:, :na], t1[:, na:2 * na], t2[:, :na], t1[:, 2 * na:], t3, t2[:, na:])
    qia_ref[...] = acc[:, :w]
    kia_ref[...] = acc[:, w:]
    wh, wmid, wlo = wiwh_ref[...], wiwm_ref[...], wiwl_ref[...]
    iwt_ref[...] = _sum6(_nt(wh, xb), _nt(wmid, xb), _nt(wh, xm), _nt(wlo, xb), _nt(wh, xl),
                         _nt(wmid, xm))

    def mm(j):
        return _mm(xb, wm_ref[:, j * w:(j + 1) * w])

    aq_ref[...] = mm(0)
    lb = lb_ref[...]
    f_ref[...] = lb + (1.0 - lb) * jax.nn.sigmoid(mm(1))
    ai_ref[...] = mm(2)
    ag = mm(3)
    gate_ref[...] = ag * jax.nn.sigmoid(ag)
    qb_ref[...] = (mm(4) * ATTN_SCALE).astype(BF16)
    kb_ref[...] = mm(5).astype(BF16)
    kvt = _nt(wkvt_ref[...], xb)
    kt_ref[0] = kvt[:w]
    vt = kvt[w:]
    vt_ref[0] = vt
    for r in range(tm // K_BLOCK):
        for p in range(B_HEADS // 2):
            vtb_ref[r, p] = vt[p * LANES:(p + 1) * LANES, r * K_BLOCK:(r + 1) * K_BLOCK].astype(BF16)
    kit_ref[0] = _nt(wikt_ref[...], xb)
    d = ga_ref.shape[1]
    ga_ref[...] = jax.nn.sigmoid(_mm(xb, wg_ref[:, :d]))
    gb_ref[...] = jax.nn.sigmoid(_mm(xb, wg_ref[:, d:]))


def _proj(x, wm, wkvt, wikt, wacc, wiw_h, wiw_m, wiw_l, wg, lb_row, *, batch, seq, tm):
    n, d = x.shape
    w = A_HEADS * A_DK
    nb = n // K_BLOCK
    tps = seq // tm
    row = lambda width: pl.BlockSpec((tm, width), lambda i: (i, 0))
    tmajor = lambda rows: pl.BlockSpec((1, rows, tm), lambda i: (i // tps, 0, i % tps))
    out_shape = (
        jax.ShapeDtypeStruct((n, w), F32),
        jax.ShapeDtypeStruct((n, w), F32),
        jax.ShapeDtypeStruct((n, w), F32),
        jax.ShapeDtypeStruct((n, w), F32),
        jax.ShapeDtypeStruct((n, w), BF16),
        jax.ShapeDtypeStruct((batch, w, seq), F32),
        jax.ShapeDtypeStruct((n, w), BF16),
        jax.ShapeDtypeStruct((batch, w, seq), F32),
        jax.ShapeDtypeStruct((nb, B_HEADS // 2, LANES, K_BLOCK), BF16),
        jax.ShapeDtypeStruct((batch, IDX_DIM, seq), F32),
        jax.ShapeDtypeStruct((n, w), F32),
        jax.ShapeDtypeStruct((n, 2 * IDX_DIM), F32),
        jax.ShapeDtypeStruct((IDX_HEADS, n), F32),
        jax.ShapeDtypeStruct((n, d), F32),
        jax.ShapeDtypeStruct((n, d), F32),
    )
    out_specs = (
        row(w), row(w), row(w), row(w), row(w), tmajor(w), row(w), tmajor(w),
        pl.BlockSpec((tm // K_BLOCK, B_HEADS // 2, LANES, K_BLOCK), lambda i: (i, 0, 0, 0)),
        tmajor(IDX_DIM), row(w), row(2 * IDX_DIM),
        pl.BlockSpec((IDX_HEADS, tm), lambda i: (0, i)),
        row(d), row(d),
    )
    consts = (wm, wkvt, wikt, wacc, wiw_h, wiw_m, wiw_l, wg, lb_row)
    return pl.pallas_call(
        functools.partial(_proj_kernel, tm=tm),
        grid=(n // tm,),
        in_specs=[row(d)] + [_const_spec(a.shape) for a in consts],
        out_specs=out_specs,
        out_shape=out_shape,
        compiler_params=_params("arbitrary"),
        name="proj",
    )(x, *consts)


def _hgrn_kernel(aq_ref, f_ref, ai_ref, gate_ref, gn_ref, o_ref, s_ref,
                 st_ref, *head_scratch):
    C, BS = HGRN_CHUNK, HGRN_BLOCK
    c = pl.program_id(1)
    per_head = [head_scratch[5 * h:5 * h + 5] for h in range(A_HEADS)]

    @pl.when(c == 0)
    def _():
        st_ref[...] = jnp.zeros_like(st_ref)
        for kpad, gpad, vpad, _, _ in per_head:
            kpad[...] = jnp.zeros_like(kpad)
            gpad[...] = jnp.zeros_like(gpad)
            vpad[...] = jnp.zeros_like(vpad)

    ri = lax.broadcasted_iota(I32, (C, C), 0)
    ci = lax.broadcasted_iota(I32, (C, C), 1)
    tri = jnp.where(ci <= ri, 1.0, 0.0).astype(BF16)

    row = lax.broadcasted_iota(I32, (C, A_DK), 0)
    blk = row // BS
    rr = ri // BS
    cc = ci // BS
    mask1 = (rr >= 2) & (cc < 2)
    mask2 = ((rr == 1) & (cc == 0)) | ((rr == 3) & (cc == 2))
    ones_bf = jnp.ones((A_DK, LANES), BF16)
    neg_inf = -jnp.inf

    for h in range(A_HEADS):
        sl = slice(h * A_DK, (h + 1) * A_DK)
        q = aq_ref[:, sl]
        f = f_ref[:, sl]
        v = ai_ref[:, sl]
        g = jnp.log(f)
        k = 1.0 - f
        g_hi = g.astype(BF16)
        r1 = g - g_hi.astype(F32)
        g_mid = r1.astype(BF16)
        g_lo = (r1 - g_mid.astype(F32)).astype(BF16)
        cs = _mm(tri, jnp.concatenate([g_hi, g_mid, g_lo], axis=1))
        G = cs[:, :A_DK] + cs[:, A_DK:2 * A_DK] + cs[:, 2 * A_DK:]
        g15 = G[BS - 1:BS]
        g31 = G[2 * BS - 1:2 * BS]
        g47 = G[3 * BS - 1:3 * BS]
        g_last = G[C - 1:C]

        v_bf = v.astype(BF16)
        st = st_ref[h]
        o = _nt((q * jnp.exp(G)).astype(BF16), st.astype(BF16))

        qe1 = jnp.exp(jnp.where(blk >= 2, G - g31, neg_inf))
        ke1 = jnp.exp(jnp.where(blk < 2, g31 - G, neg_inf))
        ref2 = jnp.where(blk < 2, g15, g47)
        qe2 = jnp.exp(jnp.where((blk == 1) | (blk == 3), G - ref2, neg_inf))
        ke2 = jnp.exp(jnp.where((blk == 0) | (blk == 2), ref2 - G, neg_inf))
        a1 = _nt((q * qe1).astype(BF16), (k * ke1).astype(BF16))
        a2 = _nt((q * qe2).astype(BF16), (k * ke2).astype(BF16))
        a = jnp.where(mask1, a1, 0.0) + jnp.where(mask2, a2, 0.0)
        o = o + _mm(a.astype(BF16), v_bf)

        kpad, gpad, vpad, dstack, rsum_s = per_head[h]
        kpad[BS:, :] = k
        gpad[BS:, :] = G
        vpad[BS:, :] = v
        for d in range(BS):
            ks = kpad[BS - d:BS - d + C, :]
            gs = gpad[BS - d:BS - d + C, :]
            e = jnp.exp(jnp.where((row % BS) >= d, G - gs, neg_inf))
            dstack[d * C:(d + 1) * C, :] = (q * ks * e).astype(BF16)
        rsum_s[...] = _mm(dstack[...], ones_bf)
        for d in range(BS):
            o = o + rsum_s[d * C:(d + 1) * C, :] * vpad[BS - d:BS - d + C, :]

        kt = (k * jnp.exp(g_last - G)).astype(BF16)
        st_new = st * jnp.exp(g_last) + _tn(v_bf, kt)
        st_ref[h] = st_new

        ms = jnp.mean(o * o, axis=-1, keepdims=True)
        on = o * lax.rsqrt(ms + RMS_EPS) * gn_ref[...]
        o_ref[:, sl] = (on * gate_ref[:, sl]).astype(BF16)

    @pl.when(c == pl.num_programs(1) - 1)
    def _():
        for h in range(A_HEADS):
            s_ref[0, h] = st_ref[h].T


def _hgrn_prompt(aq, f, ai, gate, gn_row, *, batch, seq):
    n, w = aq.shape
    nc = seq // HGRN_CHUNK
    blk = pl.BlockSpec((HGRN_CHUNK, w), lambda b, c: (b * nc + c, 0))
    pad_rows = HGRN_BLOCK + HGRN_CHUNK
    return pl.pallas_call(
        _hgrn_kernel,
        grid=(batch, nc),
        in_specs=[blk, blk, blk, blk, _const_spec(gn_row.shape)],
        out_specs=(blk, pl.BlockSpec((1, A_HEADS, A_DK, A_DV), lambda b, c: (b, 0, 0, 0))),
        out_shape=(jax.ShapeDtypeStruct((n, w), BF16),
                   jax.ShapeDtypeStruct((batch, A_HEADS, A_DK, A_DV), F32)),
        scratch_shapes=[pltpu.VMEM((A_HEADS, A_DV, A_DK), F32)] + A_HEADS * [
            pltpu.VMEM((pad_rows, A_DK), F32),
            pltpu.VMEM((pad_rows, A_DK), F32),
            pltpu.VMEM((pad_rows, A_DV), F32),
            pltpu.VMEM((HGRN_BLOCK * HGRN_CHUNK, A_DK), BF16),
            pltpu.VMEM((HGRN_BLOCK * HGRN_CHUNK, LANES), F32)],
        compiler_params=_params("arbitrary", "arbitrary"),
        name="hgrn_prompt",
    )(aq, f, ai, gate, gn_row)


def _count_rows(key_s, nkt, hits, n_out=1):
    def body(kt, accs):
        off = pl.multiple_of(kt * K_TILE, K_TILE)
        hs = hits(key_s[pl.ds(off, K_TILE), :], kt)
        return tuple(a + h.reshape(K_TILE // SUBLANES, SUBLANES, Q_BLOCK).sum(axis=0)
                     for a, h in zip(accs, hs))
    accs = lax.fori_loop(0, nkt, body, tuple(jnp.zeros((SUBLANES, Q_BLOCK), I32) for _ in range(n_out)))
    return tuple(a.sum(axis=0, keepdims=True) for a in accs)


def _swap_halves(x):
    return jnp.concatenate([x[:, B_HDIM:], x[:, :B_HDIM]], axis=1)


def _attn_kernel(q_ref, qia_ref, iwt_ref, k_ref, kia_ref, vt_ref, o_ref,
                 kaug, kst1, kst2, r1_s, r2_s, rs_s, key_s, bias_s, lg_s, ot_s, *, seq, topk):
    i = pl.program_id(1)
    nkt = (i * Q_BLOCK) // K_TILE + 1
    npair = B_HEADS // 2
    lane = lax.broadcasted_iota(I32, (Q_BLOCK, LANES), 1)
    krow = lax.broadcasted_iota(I32, (K_TILE, Q_BLOCK), 0)
    qpos = i * Q_BLOCK + lax.broadcasted_iota(I32, (K_TILE, Q_BLOCK), 1)

    @pl.when(i == 0)
    def _():
        def body(kb, carry):
            off = pl.multiple_of(kb * K_BLOCK, K_BLOCK)
            pos = off + lax.broadcasted_iota(I32, (K_BLOCK, LANES), 0)
            ln = lax.broadcasted_iota(I32, (K_BLOCK, LANES), 1)
            feat = jnp.where(ln == 0, (pos // 64) * 64, jnp.where(ln == 1, pos % 64, 0))
            feat = feat.astype(F32).astype(BF16)
            for p in range(npair):
                kaug[p, pl.ds(off, K_BLOCK), :LANES] = k_ref[pl.ds(off, K_BLOCK), p * LANES:(p + 1) * LANES]
                kaug[p, pl.ds(off, K_BLOCK), LANES:] = feat
            kh, km, kl = _split3(kia_ref[pl.ds(off, K_BLOCK), :])
            half = ln < IDX_DIM
            kst1[pl.ds(off, K_BLOCK), :LANES] = kh
            kst1[pl.ds(off, K_BLOCK), LANES:] = jnp.where(half, km, kh)
            kst2[pl.ds(off, K_BLOCK), :] = jnp.where(half, kl, km)
            return carry
        lax.fori_loop(0, seq // K_BLOCK, body, 0)

    zero_bf = jnp.zeros((Q_BLOCK, LANES), BF16)
    for p in range(npair):
        qp = q_ref[:, p * LANES:(p + 1) * LANES]
        lo = lane < B_HDIM
        qh, qm, ql = _split3(qia_ref[:, p * LANES:(p + 1) * LANES])
        rqh = _swap_halves(qh)
        a0 = jnp.where(lo, qh, _swap_halves(qm))
        b0 = jnp.where(lo, rqh, qm)
        r1_s[p, :Q_BLOCK, :LANES] = a0
        r1_s[p, :Q_BLOCK, LANES:] = jnp.where(lo, qh, _swap_halves(ql))
        r1_s[p, Q_BLOCK:, :LANES] = b0
        r1_s[p, Q_BLOCK:, LANES:] = jnp.where(lo, rqh, ql)
        r2_s[p, :Q_BLOCK, :] = a0
        r2_s[p, Q_BLOCK:, :] = b0
        rs_s[p, :Q_BLOCK, :LANES] = jnp.where(lo, qp, zero_bf)
        rs_s[p, Q_BLOCK:, :LANES] = jnp.where(lo, zero_bf, qp)
        m0 = 2.0 ** -(2 * p + 1)
        m1 = 2.0 ** -(2 * p + 2)
        rs_s[p, :Q_BLOCK, LANES:] = jnp.where(lane < 2, m0, 0.0).astype(BF16)
        rs_s[p, Q_BLOCK:, LANES:] = jnp.where(lane < 2, m1, 0.0).astype(BF16)

    def score_body(kt, carry):
        off = pl.multiple_of(kt * K_TILE, K_TILE)
        k1 = kst1[pl.ds(off, K_TILE), :]
        k2 = kst2[pl.ds(off, K_TILE), :]
        acc = jnp.zeros((K_TILE, Q_BLOCK), F32)
        for p in range(npair):
            s2 = _nt(k1, r1_s[p]) + _nt(k2, r2_s[p])
            for hh in range(2):
                hd = 2 * p + hh
                acc = acc + iwt_ref[hd:hd + 1, :] * jnp.maximum(s2[:, hh * Q_BLOCK:(hh + 1) * Q_BLOCK], 0.0)
        sc = jnp.where(off + krow <= qpos, acc * IDX_SCALE, -jnp.inf)
        key_s[pl.ds(off, K_TILE), :] = _float_key(sc)
        return carry
    lax.fori_loop(0, nkt, score_body, 0)

    def bit_body(it, ans):
        cand = ans | lax.shift_left(jnp.int32(1), 31 - it)
        cnt, = _count_rows(key_s, nkt, lambda blk, kt: (jnp.where(blk >= (cand ^ INT_MIN), 1, 0),))
        return jnp.where(cnt >= topk, cand, ans)
    ans = lax.fori_loop(0, 32, bit_body, jnp.zeros((1, Q_BLOCK), I32))
    thr = ans ^ INT_MIN
    thr_gt = jnp.maximum(thr, NEG_INF_KEY)
    n_gt, n_eq = _count_rows(
        key_s, nkt, lambda blk, kt: (jnp.where(blk > thr_gt, 1, 0), jnp.where(blk == thr, 1, 0)), n_out=2)
    need = topk - n_gt

    def tie_search():
        def tie_body(it, y):
            cand = y | lax.shift_left(jnp.int32(1), 11 - it)
            cnt, = _count_rows(
                key_s, nkt,
                lambda blk, kt: (jnp.where(blk == thr, jnp.where(kt * K_TILE + krow < cand, 1, 0), 0),))
            return jnp.where(cnt < need, cand, y)
        return lax.fori_loop(0, 12, tie_body, jnp.zeros((1, Q_BLOCK), I32))
    surplus = jnp.max(jnp.where(thr > NEG_INF_KEY, n_eq - need, 0))
    y = lax.cond(surplus > 0, tie_search, lambda: jnp.full((1, Q_BLOCK), 4095, I32))
    ylim = jnp.where(thr <= NEG_INF_KEY, 0, y + 1)

    def bias_body(kt, carry):
        off = pl.multiple_of(kt * K_TILE, K_TILE)
        blk = key_s[pl.ds(off, K_TILE), :]
        tie = jnp.where(blk == thr, jnp.where(off + krow < ylim, 0.0, -jnp.inf), -jnp.inf)
        bias_s[pl.ds(off, K_TILE), :] = jnp.where(blk > thr_gt, 0.0, tie)
        return carry
    lax.fori_loop(0, nkt, bias_body, 0)

    def fold(x, op):
        return op(x.reshape(x.shape[0] // SUBLANES, SUBLANES, 2 * Q_BLOCK), axis=0)

    def pass1(kt, mxs):
        off = pl.multiple_of(kt * K_TILE, K_TILE)
        b = bias_s[pl.ds(off, K_TILE), :]
        b2 = jnp.concatenate([b, b], axis=1)
        out = []
        for p in range(npair):
            lg = _nt(kaug[p, pl.ds(off, K_TILE), :], rs_s[p]) + b2
            lg_s[p, pl.ds(off, K_TILE), :] = lg
            out.append(jnp.maximum(mxs[p], fold(lg, jnp.max)))
        return tuple(out)
    mxs = lax.fori_loop(0, nkt, pass1, tuple(jnp.full((SUBLANES, 2 * Q_BLOCK), -jnp.inf, F32)
                                             for _ in range(npair)))
    ms = [mx.max(axis=0, keepdims=True) for mx in mxs]

    ot_s[...] = jnp.zeros_like(ot_s)

    def pass2(kt, l8s):
        out = list(l8s)
        for u in range(K_TILE // K_BLOCK):
            kb = kt * (K_TILE // K_BLOCK) + u
            off = pl.multiple_of(kb * K_BLOCK, K_BLOCK)
            for p in range(npair):
                pe = jnp.exp(lg_s[p, pl.ds(off, K_BLOCK), :] - ms[p])
                ot_s[p] += _mm(vt_ref[kb, p], pe.astype(BF16))
                out[p] = out[p] + fold(pe, jnp.sum)
        return tuple(out)
    l8s = lax.fori_loop(0, nkt, pass2, tuple(jnp.zeros((SUBLANES, 2 * Q_BLOCK), F32) for _ in range(npair)))
    for p in range(npair):
        inv = 1.0 / l8s[p].sum(axis=0, keepdims=True)
        ot = ot_s[p]
        top = (ot[:B_HDIM, :Q_BLOCK] * inv[:, :Q_BLOCK]).T
        bot = (ot[B_HDIM:, Q_BLOCK:] * inv[:, Q_BLOCK:]).T
        o_ref[:, p * LANES:(p + 1) * LANES] = jnp.concatenate([top, bot], axis=1).astype(BF16)


def _attn_prompt(qb, qia, iwt, kb, kia, vtb, *, batch, seq, topk):
    n, w = qb.shape
    nq = seq // Q_BLOCK
    npair = B_HEADS // 2
    assert seq % K_TILE == 0 and K_TILE % K_BLOCK == 0
    qspec = pl.BlockSpec((Q_BLOCK, w), lambda b, i: (b * nq + i, 0))
    return pl.pallas_call(
        functools.partial(_attn_kernel, seq=seq, topk=topk),
        grid=(batch, nq),
        in_specs=[qspec, qspec,
                  pl.BlockSpec((IDX_HEADS, Q_BLOCK), lambda b, i: (0, b * nq + i)),
                  pl.BlockSpec((seq, w), lambda b, i: (b, 0)),
                  pl.BlockSpec((seq, 2 * IDX_DIM), lambda b, i: (b, 0)),
                  pl.BlockSpec((seq // K_BLOCK, npair, LANES, K_BLOCK), lambda b, i: (b, 0, 0, 0))],
        out_specs=qspec,
        out_shape=jax.ShapeDtypeStruct((n, w), BF16),
        scratch_shapes=[pltpu.VMEM((npair, seq, 2 * LANES), BF16),
                        pltpu.VMEM((seq, 2 * LANES), BF16),
                        pltpu.VMEM((seq, LANES), BF16),
                        pltpu.VMEM((npair, 2 * Q_BLOCK, 2 * LANES), BF16),
                        pltpu.VMEM((npair, 2 * Q_BLOCK, LANES), BF16),
                        pltpu.VMEM((npair, 2 * Q_BLOCK, 2 * LANES), BF16),
                        pltpu.VMEM((seq, Q_BLOCK), I32),
                        pltpu.VMEM((seq, Q_BLOCK), F32),
                        pltpu.VMEM((npair, seq, 2 * Q_BLOCK), F32),
                        pltpu.VMEM((npair, LANES, 2 * Q_BLOCK), F32)],
        compiler_params=_params("arbitrary", "arbitrary"),
        name="attn_prompt",
    )(qb, qia, iwt, kb, kia, vtb)


def _layer_norm(x, g, b):
    mu = jnp.mean(x, axis=-1, keepdims=True)
    xc = x - mu
    var = jnp.mean(xc * xc, axis=-1, keepdims=True)
    return xc * lax.rsqrt(var + LN_EPS) * g + b


def _final_kernel(x_ref, oa_ref, ob_ref, ga_ref, gb_ref, wua_ref, wub_ref, wo_ref,
                  l1g_ref, l1b_ref, wfg_ref, wfu_ref, wfd_ref, l2g_ref, l2b_ref, y_ref,
                  *, alpha, ff_chunk):
    ya = _mm(oa_ref[...], wua_ref[...])
    yb = _mm(ob_ref[...], wub_ref[...])
    merged = (ga_ref[...] * ya + gb_ref[...] * yb).astype(BF16)
    mix = _mm(merged, wo_ref[...])
    x1 = _layer_norm(alpha * x_ref[...] + mix, l1g_ref[...], l1b_ref[...])
    x1b = x1.astype(BF16)
    d_ff = wfg_ref.shape[1]
    acc = jnp.zeros(x1.shape, F32)
    for c in range(d_ff // ff_chunk):
        cs = slice(c * ff_chunk, (c + 1) * ff_chunk)
        hg = _mm(x1b, wfg_ref[:, cs])
        hu = _mm(x1b, wfu_ref[:, cs])
        hc = (hg * jax.nn.sigmoid(hg) * hu).astype(BF16)
        acc = acc + _mm(hc, wfd_ref[cs, :])
    y_ref[...] = _layer_norm(alpha * x1 + acc, l2g_ref[...], l2b_ref[...])


def _final(x, oa, ob, ga, gb, wua, wub, wo, l1g, l1b, wfg, wfu, wfd, l2g, l2b, *, tm, alpha):
    n, d = x.shape
    w = oa.shape[1]
    row = lambda width: pl.BlockSpec((tm, width), lambda i: (i, 0))
    consts = (wua, wub, wo, l1g, l1b, wfg, wfu, wfd, l2g, l2b)
    return pl.pallas_call(
        functools.partial(_final_kernel, alpha=alpha, ff_chunk=256),
        grid=(n // tm,),
        in_specs=[row(d), row(w), row(w), row(d), row(d)] + [_const_spec(a.shape) for a in consts],
        out_specs=row(d),
        out_shape=jax.ShapeDtypeStruct((n, d), F32),
        compiler_params=_params("arbitrary"),
        name="merge_ffn",
    )(x, oa, ob, ga, gb, *consts)


def _hgrn_step_kernel(qt_ref, ft_ref, v_ref, gate_ref, gn_ref, s_ref, so_ref, o_ref, *, per_step):
    for j in range(per_step):
        b = pl.program_id(0) * per_step + j
        qcol = _lane_column(qt_ref[...], b)
        fcol = _lane_column(ft_ref[...], b)
        for h in range(A_HEADS):
            sl = slice(h * A_DK, (h + 1) * A_DK)
            fc = fcol[sl]
            vrow = v_ref[j, :, sl]
            s_new = fc * s_ref[j, h] + (1.0 - fc) * vrow
            so_ref[j, h] = s_new
            o = jnp.sum(qcol[sl] * s_new, axis=0, keepdims=True)
            ms = jnp.mean(o * o, axis=-1, keepdims=True)
            on = o * lax.rsqrt(ms + RMS_EPS) * gn_ref[...]
            o_ref[j, :, sl] = (on * gate_ref[j, :, sl]).astype(BF16)


def _hgrn_step(aq_t, f_t, ai3, gate3, gn_row, state):
    nb = state.shape[0]
    w = A_HEADS * A_DK
    per_step = 8 if nb % 8 == 0 else 1
    sspec = pl.BlockSpec((per_step, A_HEADS, A_DK, A_DV), lambda b: (b, 0, 0, 0))
    rspec = pl.BlockSpec((per_step, 1, w), lambda b: (b, 0, 0))
    return pl.pallas_call(
        functools.partial(_hgrn_step_kernel, per_step=per_step),
        grid=(nb // per_step,),
        in_specs=[_const_spec(aq_t.shape), _const_spec(f_t.shape), rspec, rspec,
                  _const_spec(gn_row.shape), sspec],
        out_specs=(sspec, rspec),
        out_shape=(jax.ShapeDtypeStruct(state.shape, F32),
                   jax.ShapeDtypeStruct((nb, 1, w), BF16)),
        compiler_params=_params("arbitrary"),
        name="hgrn_step",
    )(aq_t, f_t, ai3, gate3, gn_row, state)


def _page_copies(pt_ref, cache_ref, buf, sem, b, slot, n_pages, layer):
    def body(pg, carry):
        pltpu.make_async_copy(cache_ref.at[layer, pt_ref[b, pg]], buf.at[slot, pg], sem.at[slot]).start()
        return carry
    lax.fori_loop(0, n_pages, body, 0)


def _idx_score_kernel(pt_ref, qi_ref, iwt_ref, kn_ref, cache_ref, sc_ref, sn_ref, buf, sem,
                      *, n_pages, layer):
    b = pl.program_id(0)
    nb = pl.num_programs(0)
    slot = b % 2

    @pl.when(b == 0)
    def _():
        _page_copies(pt_ref, cache_ref, buf, sem, b, slot, n_pages, layer)

    @pl.when(b + 1 < nb)
    def _():
        _page_copies(pt_ref, cache_ref, buf, sem, b + 1, 1 - slot, n_pages, layer)

    qi = qi_ref[0]
    wcol = _lane_column(iwt_ref[...], b)

    def head_sum(s):
        return jnp.sum(wcol * jnp.maximum(s, 0.0), axis=0, keepdims=True) * IDX_SCALE

    sn = jnp.sum(qi * kn_ref[0][:, :IDX_DIM], axis=1, keepdims=True)
    sn_ref[0] = jnp.broadcast_to(head_sum(sn), (1, LANES))

    qh, qm, ql = _split3(qi)
    lhs1 = jnp.concatenate([qh, qm, qh, ql], axis=1)
    lhs2 = jnp.concatenate([qh, qm], axis=1)

    def page_scores(tile):
        kh, km, kl = _split3(tile)
        s = _mm(lhs1, jnp.concatenate([kh, kh, km, kh], axis=0))
        return head_sum(s + _mm(lhs2, jnp.concatenate([kl, km], axis=0)))

    pltpu.make_async_copy(buf.at[slot], buf.at[slot], sem.at[slot]).wait()
    for g in range(n_pages // SUBLANES):
        rows = [page_scores(buf[slot, g * SUBLANES + r]) for r in range(SUBLANES)]
        sc_ref[0, g * SUBLANES:(g + 1) * SUBLANES, :] = jnp.concatenate(rows, axis=0)


def _idx_scores(page_table, qi3, iwt, kidx_new3, cache_kidx_t, *, layer):
    nb, n_pages = page_table.shape
    assert n_pages % SUBLANES == 0
    grid_spec = pltpu.PrefetchScalarGridSpec(
        num_scalar_prefetch=1,
        grid=(nb,),
        in_specs=[pl.BlockSpec((1, IDX_HEADS, IDX_DIM), lambda b, pt: (b, 0, 0)),
                  _const_spec(iwt.shape),
                  pl.BlockSpec((1, 1, 2 * IDX_DIM), lambda b, pt: (b, 0, 0)),
                  pl.BlockSpec(memory_space=pl.ANY)],
        out_specs=(pl.BlockSpec((1, n_pages, PAGE_SIZE), lambda b, pt: (b, 0, 0)),
                   pl.BlockSpec((1, 1, LANES), lambda b, pt: (b, 0, 0))),
        scratch_shapes=[pltpu.VMEM((2, n_pages, IDX_DIM, PAGE_SIZE), F32),
                        pltpu.SemaphoreType.DMA((2,))],
    )
    return pl.pallas_call(
        functools.partial(_idx_score_kernel, n_pages=n_pages, layer=layer),
        grid_spec=grid_spec,
        out_shape=(jax.ShapeDtypeStruct((nb, n_pages, PAGE_SIZE), F32),
                   jax.ShapeDtypeStruct((nb, 1, LANES), F32)),
        compiler_params=_params("arbitrary"),
        name="idx_scores",
    )(page_table, qi3, iwt, kidx_new3, cache_kidx_t)


def _select_kernel(sc_ref, sn_ref, mask_ref, newsel_ref, key_s, *, topk, pos_bits):
    nb, past = sc_ref.shape
    ncol = past // LANES
    key_s[...] = _float_key(sc_ref[...])
    kn = _float_key(sn_ref[...])
    pos = lax.broadcasted_iota(I32, (nb, LANES), 1)

    def count(hits_past, hit_new):
        def body(j, acc):
            off = pl.multiple_of(j * LANES, LANES)
            return acc + hits_past(key_s[:, pl.ds(off, LANES)], off)
        acc = lax.fori_loop(0, ncol, body, jnp.zeros((nb, LANES), I32))
        tot = jnp.sum(acc, axis=1, keepdims=True)
        return jnp.broadcast_to(tot, (nb, LANES)) + hit_new

    def bit_body(it, ans):
        cand = ans | lax.shift_left(jnp.int32(1), 31 - it)
        c = cand ^ INT_MIN
        cnt = count(lambda blk, off: jnp.where(blk >= c, 1, 0), jnp.where(kn >= c, 1, 0))
        return jnp.where(cnt >= topk, cand, ans)
    ans = lax.fori_loop(0, 32, bit_body, jnp.zeros((nb, LANES), I32))
    thr = ans ^ INT_MIN
    thr_gt = jnp.maximum(thr, NEG_INF_KEY)
    need = topk - count(lambda blk, off: jnp.where(blk > thr_gt, 1, 0), jnp.where(kn > thr_gt, 1, 0))
    n_eq = count(lambda blk, off: jnp.where(blk == thr, 1, 0), jnp.where(kn == thr, 1, 0))

    def tie_search():
        def tie_body(it, y):
            cand = y | lax.shift_left(jnp.int32(1), pos_bits - 1 - it)
            cnt = count(lambda blk, off: jnp.where(blk == thr, jnp.where(off + pos < cand, 1, 0), 0),
                        jnp.where(kn == thr, jnp.where(past < cand, 1, 0), 0))
            return jnp.where(cnt < need, cand, y)
        return lax.fori_loop(0, pos_bits, tie_body, jnp.zeros((nb, LANES), I32))
    surplus = jnp.max(jnp.where(thr > NEG_INF_KEY, n_eq - need, 0))
    y = lax.cond(surplus > 0, tie_search, lambda: jnp.full((nb, LANES), 2 ** pos_bits - 1, I32))
    ylim = jnp.where(thr <= NEG_INF_KEY, 0, y + 1)

    def out_body(j, carry):
        off = pl.multiple_of(j * LANES, LANES)
        blk = key_s[:, pl.ds(off, LANES)]
        tie = jnp.where(blk == thr, jnp.where(off + pos < ylim, 1.0, 0.0), 0.0)
        mask_ref[:, pl.ds(off, LANES)] = jnp.where(blk > thr_gt, 1.0, tie)
        return carry
    lax.fori_loop(0, ncol, out_body, 0)
    tie_new = jnp.where(kn == thr, jnp.where(past < ylim, 1, 0), 0)
    newsel_ref[...] = jnp.where(kn > thr_gt, 1, tie_new)


def _select(scores, snew, *, topk):
    nb, past = scores.shape
    pos_bits = (past + 1).bit_length()
    return pl.pallas_call(
        functools.partial(_select_kernel, topk=topk, pos_bits=pos_bits),
        out_shape=(jax.ShapeDtypeStruct((nb, past), F32),
                   jax.ShapeDtypeStruct((nb, LANES), I32)),
        scratch_shapes=[pltpu.VMEM((nb, past), I32)],
        compiler_params=pltpu.CompilerParams(vmem_limit_bytes=VMEM_LIMIT_BYTES),
        name="topk_select",
    )(scores, snew)


def _paged_attn_kernel(pt_ref, qt_ref, knt_ref, vnt_ref, mask_ref, nsel_ref, ck_ref, cv_ref, ot_ref,
                       kring, vring, sem, qcol_s, p_s, acc_s, *, n_pages, ring, layer):
    b = pl.program_id(0)
    nb = pl.num_programs(0)
    past = n_pages * PAGE_SIZE

    def page_copy(cache_ref, buf, which, bb, pg):
        return pltpu.make_async_copy(cache_ref.at[layer, pt_ref[bb, pg]], buf.at[pg % ring],
                                     sem.at[which, pg % ring])

    def start_ahead(cache_ref, buf, which, pg):
        n = pg + ring
        wrap = n >= n_pages
        bb = jnp.where(wrap, b + 1, b)
        pg2 = jnp.where(wrap, n - n_pages, n)

        @pl.when(bb < nb)
        def _():
            page_copy(cache_ref, buf, which, bb, pg2).start()

    @pl.when(b == 0)
    def _():
        ot_ref[...] = jnp.zeros_like(ot_ref)
        for r in range(ring):
            page_copy(ck_ref, kring, 0, 0, r).start()
            page_copy(cv_ref, vring, 1, 0, r).start()

    qcol = _lane_column(qt_ref[...], b)
    kncol = _lane_column(knt_ref[...], b)
    vncol = _lane_column(vnt_ref[...], b)
    for h in range(B_HEADS):
        qcol_s[h] = jnp.broadcast_to(qcol[h * B_HDIM:(h + 1) * B_HDIM], (B_HDIM, PAGE_SIZE))

    sub = lax.broadcasted_iota(I32, (B_HEADS, PAGE_SIZE), 0)
    lane = lax.broadcasted_iota(I32, (B_HEADS, PAGE_SIZE), 1)
    slopes = jnp.exp2(-(sub + 1).astype(F32))

    def head_rows(tile_of_head):
        out = jnp.zeros((B_HEADS, PAGE_SIZE), F32)
        for h in range(B_HEADS):
            out = jnp.where(sub == h, jnp.sum(tile_of_head(h), axis=0, keepdims=True), out)
        return out

    def k_body(pg, mx):
        page_copy(ck_ref, kring, 0, b, pg).wait()
        lg = head_rows(lambda h: kring[pg % ring, h] * qcol_s[h])
        dist = (past - pg * PAGE_SIZE - lane).astype(F32)
        lg = jnp.where(mask_ref[0, pl.ds(pg, 1), :] > 0.0, lg - slopes * dist, -jnp.inf)
        p_s[pg] = lg
        start_ahead(ck_ref, kring, 0, pg)
        return jnp.maximum(mx, lg)
    mx = lax.fori_loop(0, n_pages, k_body, jnp.full((B_HEADS, PAGE_SIZE), -jnp.inf, F32))

    lgn = head_rows(lambda h: qcol_s[h] * kncol[h * B_HDIM:(h + 1) * B_HDIM])
    lgn = jnp.where(nsel_ref[pl.ds(b, 1), :] > 0, lgn, -jnp.inf)
    m = jnp.maximum(jnp.max(mx, axis=1, keepdims=True), lgn)
    pn = jnp.exp(lgn - m)

    def e_body(pg, l):
        pe = jnp.exp(p_s[pg] - m)
        p_s[pg] = pe
        return l + pe
    lsum = lax.fori_loop(0, n_pages, e_body, jnp.zeros((B_HEADS, PAGE_SIZE), F32))
    inv = 1.0 / (jnp.sum(lsum, axis=1, keepdims=True) + pn)

    acc_s[...] = jnp.zeros_like(acc_s)

    def v_body(pg, carry):
        page_copy(cv_ref, vring, 1, b, pg).wait()
        pe = p_s[pg]
        for h in range(B_HEADS):
            acc_s[h] += vring[pg % ring, h] * pe[h:h + 1, :]
        start_ahead(cv_ref, vring, 1, pg)
        return carry
    lax.fori_loop(0, n_pages, v_body, 0)

    lane_o = lax.broadcasted_iota(I32, (B_HDIM, ot_ref.shape[1]), 1)
    for h in range(B_HEADS):
        hs = slice(h * B_HDIM, (h + 1) * B_HDIM)
        oc = jnp.sum(acc_s[h], axis=1, keepdims=True) + pn[h:h + 1, :1] * vncol[hs]
        oc = oc * inv[h:h + 1, :1]
        ot_ref[hs, :] = jnp.where(lane_o == b, oc, ot_ref[hs, :])


def _paged_attn(page_table, q_t, kn_t, vn_t, mask3, newsel, cache_k_t, cache_v_t, *, layer):
    nb, n_pages = page_table.shape
    ring = min(PAGE_RING, n_pages)
    assert n_pages % ring == 0
    w = B_HEADS * B_HDIM
    page = (B_HEADS, B_HDIM, PAGE_SIZE)
    grid_spec = pltpu.PrefetchScalarGridSpec(
        num_scalar_prefetch=1,
        grid=(nb,),
        in_specs=[_const_spec(q_t.shape), _const_spec(kn_t.shape), _const_spec(vn_t.shape),
                  pl.BlockSpec((1, n_pages, PAGE_SIZE), lambda b, pt: (b, 0, 0)),
                  _const_spec(newsel.shape),
                  pl.BlockSpec(memory_space=pl.ANY),
                  pl.BlockSpec(memory_space=pl.ANY)],
        out_specs=pl.BlockSpec((w, nb), lambda b, pt: (0, 0)),
        scratch_shapes=[pltpu.VMEM((ring,) + page, F32),
                        pltpu.VMEM((ring,) + page, F32),
                        pltpu.SemaphoreType.DMA((2, ring)),
                        pltpu.VMEM(page, F32),
                        pltpu.VMEM((n_pages, B_HEADS, PAGE_SIZE), F32),
                        pltpu.VMEM(page, F32)],
    )
    return pl.pallas_call(
        functools.partial(_paged_attn_kernel, n_pages=n_pages, ring=ring, layer=layer),
        grid_spec=grid_spec,
        out_shape=jax.ShapeDtypeStruct((w, nb), F32),
        compiler_params=_params("arbitrary"),
        name="paged_attn",
    )(page_table, q_t, kn_t, vn_t, mask3, newsel, cache_k_t, cache_v_t)


def _prep_weights(lb, w_in, hgrn_norm_g, w_up_a, w_up_b, w_o, ln1_g, ln1_b,
                  w_ffn_gate, w_ffn_up, w_ffn_down, ln2_g, ln2_b):
    w = A_HEADS * A_DK
    o_k, o_v, o_iq = 5 * w, 6 * w, 7 * w
    o_ik = 8 * w
    o_iw = o_ik + IDX_DIM
    o_g = o_iw + IDX_HEADS
    wb = w_in.astype(BF16)
    wt = w_in.T.astype(BF16)
    row = lambda a: a.reshape(1, -1).astype(F32)

    def split3(a):
        hi = a.astype(BF16)
        r = a - hi.astype(F32)
        mid = r.astype(BF16)
        return hi, mid, (r - mid.astype(F32)).astype(BF16)

    w_idx = jnp.concatenate([w_in[:, o_iq:o_ik], w_in[:, o_ik:o_iw], w_in[:, o_ik:o_iw]], axis=1)
    wiw_h, wiw_m, wiw_l = split3(w_in[:, o_iw:o_g].T)
    return dict(
        wm=wb[:, :o_v],
        wkvt=wt[o_k:o_v + w],
        wikt=wt[o_ik:o_iw],
        wacc=jnp.concatenate(split3(w_idx), axis=1),
        wiw_h=wiw_h, wiw_m=wiw_m, wiw_l=wiw_l,
        wg=wb[:, o_g:],
        lb=row(lb),
        gn=row(hgrn_norm_g),
        wua=w_up_a.astype(BF16), wub=w_up_b.astype(BF16), wo=w_o.astype(BF16),
        l1g=row(ln1_g), l1b=row(ln1_b),
        wfg=w_ffn_gate.astype(BF16), wfu=w_ffn_up.astype(BF16), wfd=w_ffn_down.astype(BF16),
        l2g=row(ln2_g), l2b=row(ln2_b),
    )


def _proj_call(x2, pw, *, batch, seq, tm):
    return _proj(x2, pw["wm"], pw["wkvt"], pw["wikt"], pw["wacc"], pw["wiw_h"], pw["wiw_m"], pw["wiw_l"],
                 pw["wg"], pw["lb"], batch=batch, seq=seq, tm=tm)


def _final_call(x2, oa, ob, ga, gb, pw, *, tm, alpha):
    return _final(x2, oa, ob, ga, gb, pw["wua"], pw["wub"], pw["wo"], pw["l1g"], pw["l1b"],
                  pw["wfg"], pw["wfu"], pw["wfd"], pw["l2g"], pw["l2b"], tm=tm, alpha=alpha)


def _token_major(feat_t, batch, seq, heads):
    if heads is None:
        return feat_t.transpose(0, 2, 1)
    return feat_t.reshape(batch, heads, -1, seq).transpose(0, 3, 1, 2)


def _prompt_layer(x, pw, *, alpha):
    batch, seq, d = x.shape
    n = batch * seq
    x2 = x.reshape(n, d)
    (aq, f, ai, gate, qb, kt, kb, vt, vtb, kit, qia, kia, iwt, ga, gb) = _proj_call(
        x2, pw, batch=batch, seq=seq, tm=256)
    oa, s_new = _hgrn_prompt(aq, f, ai, gate, pw["gn"], batch=batch, seq=seq)
    ob = _attn_prompt(qb, qia, iwt, kb, kia, vtb, batch=batch, seq=seq, topk=min(TOPK_MAX, seq // 4))
    y = _final_call(x2, oa, ob, ga, gb, pw, tm=512, alpha=alpha)
    return (y.reshape(batch, seq, d), s_new,
            _token_major(kt, batch, seq, B_HEADS), _token_major(vt, batch, seq, B_HEADS),
            _token_major(kit, batch, seq, None))


def _sample_layer(x, state, cache_k_t, cache_v_t, cache_kidx_t, page_table, pw, *, alpha, layer):
    nb, tn, d = x.shape
    n_pages = page_table.shape[1]
    past = n_pages * PAGE_SIZE
    topk = min(TOPK_MAX, (past + tn) // 4)
    w = A_HEADS * A_DK
    x2 = x.reshape(nb, d)
    (aq, f, ai, gate, qb, kt, kb, vt, vtb, kit, qia, kia, iwt, ga, gb) = _proj_call(
        x2, pw, batch=1, seq=nb, tm=nb)
    s_new, oa3 = _hgrn_step(aq.T, f.T, ai.reshape(nb, 1, w), gate.reshape(nb, 1, w), pw["gn"], state)
    scores, snew = _idx_scores(page_table, qia.reshape(nb, IDX_HEADS, IDX_DIM), iwt,
                               kia.reshape(nb, 1, 2 * IDX_DIM), cache_kidx_t, layer=layer)
    mask, newsel = _select(scores.reshape(nb, past), snew.reshape(nb, LANES), topk=topk)
    ob_t = _paged_attn(page_table, qb.astype(F32).T, kt[0], vt[0], mask.reshape(nb, n_pages, PAGE_SIZE),
                       newsel, cache_k_t, cache_v_t, layer=layer)
    y = _final_call(x2, oa3.reshape(nb, w), ob_t.T.astype(BF16), ga, gb, pw, tm=nb, alpha=alpha)
    tok = lambda a, heads: _token_major(a, 1, nb, heads)[0][:, None]
    return (y.reshape(nb, tn, d), s_new, tok(kt, B_HEADS), tok(vt, B_HEADS), tok(kit, None))


def kernel(x_prompt, x_sample, cache_k, cache_v, cache_kidx, state_hgrn, page_table, hgrn_lb_logits,
           w_in, hgrn_norm_g, w_up_a, w_up_b, w_o, ln1_g, ln1_b, w_ffn_gate, w_ffn_up, w_ffn_down,
           ln2_g, ln2_b):
    depth = w_in.shape[0]
    alpha = (2.0 * depth) ** 0.25
    lb_all = jnp.cumsum(jax.nn.softmax(hgrn_lb_logits.astype(F32), axis=0), axis=0)
    cache_k_t = cache_k.transpose(0, 1, 3, 4, 2)
    cache_v_t = cache_v.transpose(0, 1, 3, 4, 2)
    cache_kidx_t = cache_kidx.transpose(0, 1, 3, 2)
    xp, xs = x_prompt, x_sample
    outs_p, outs_s = [], []
    for l in range(depth):
        pw = _prep_weights(lb_all[l], w_in[l], hgrn_norm_g[l], w_up_a[l], w_up_b[l], w_o[l],
                           ln1_g[l], ln1_b[l], w_ffn_gate[l], w_ffn_up[l], w_ffn_down[l],
                           ln2_g[l], ln2_b[l])
        xp, sp, kp, vp, kip = _prompt_layer(xp, pw, alpha=alpha)
        xs, ss, ks, vs, kis = _sample_layer(xs, state_hgrn[l], cache_k_t, cache_v_t, cache_kidx_t,
                                            page_table, pw, alpha=alpha, layer=l)
        outs_p.append((kp, vp, kip, sp))
        outs_s.append((ks, vs, kis, ss))
    stack = lambda outs, j: jnp.stack([o[j] for o in outs], 0)
    return (xp, xs, stack(outs_p, 0), stack(outs_p, 1), stack(outs_p, 2), stack(outs_p, 3),
            stack(outs_s, 0), stack(outs_s, 1), stack(outs_s, 2), stack(outs_s, 3))
```
